```python
import math
import jax, jax.numpy as jnp
from jax import lax
import numpy as np

D_MODEL = 2048
BATCH = 1
SEQ = 8192
DEPTH = 4

HEAD_DIM = 128
RET_HEADS = D_MODEL // 256
RET_V_HEAD_DIM = 2 * HEAD_DIM
RET_QK = RET_HEADS * HEAD_DIM
RET_V = RET_HEADS * RET_V_HEAD_DIM
DIFF_HEADS = D_MODEL // 256
DIFF_QK = 2 * DIFF_HEADS * HEAD_DIM
DIFF_V = DIFF_HEADS * 2 * HEAD_DIM
D_FF = 4 * D_MODEL
RET_CHUNK = 128
Q_BLOCK = 128
ROPE_THETA = 10000.0
NORM_EPS = 1e-6
GN_EPS = 1e-5
SPLITS = (RET_QK, RET_QK, RET_V, RET_V, DIFF_QK, DIFF_QK, DIFF_V, D_MODEL, D_MODEL)
IN_COLS = sum(SPLITS)

kernel_name = "hybrid_retention_diffattn_encoder"


def rms_norm(x, w):
    xf = x.astype(jnp.float32)
    y = xf * lax.rsqrt(jnp.mean(xf * xf, axis=-1, keepdims=True) + NORM_EPS)
    return (y * w.astype(jnp.float32)).astype(x.dtype)


def head_group_norm(x, w):
    xf = x.astype(jnp.float32)
    mu = jnp.mean(xf, axis=-1, keepdims=True)
    var = jnp.mean(jnp.square(xf - mu), axis=-1, keepdims=True)
    return (xf - mu) * lax.rsqrt(var + GN_EPS) * w.astype(jnp.float32)


def rope_tables(L):
    inv = 1.0 / (ROPE_THETA ** (jnp.arange(0, HEAD_DIM, 2, dtype=jnp.float32) / HEAD_DIM))
    ang = jnp.arange(L, dtype=jnp.float32)[:, None] * inv[None, :]
    ang = jnp.concatenate([ang, ang], axis=-1)
    return jnp.cos(ang), jnp.sin(ang)


def apply_rope(x, cos, sin):
    half = x.shape[-1] // 2
    x1, x2 = x[..., :half], x[..., half:]
    rot = jnp.concatenate([-x2, x1], axis=-1)
    return (x * cos + rot * sin).astype(x.dtype)


def retention_direction(q, k, v, log_g):
    q = q.astype(jnp.float32); k = k.astype(jnp.float32); v = v.astype(jnp.float32)
    B, H, L, dk = q.shape
    dv = v.shape[-1]
    C = RET_CHUNK
    N = L // C
    q = q.reshape(B, H, N, C, dk)
    k = k.reshape(B, H, N, C, dk)
    v = v.reshape(B, H, N, C, dv)
    i = jnp.arange(C, dtype=jnp.float32)
    diff = i[:, None] - i[None, :]
    causal_in_chunk = diff >= 0
    intra_decay = jnp.where(causal_in_chunk[None],
                            jnp.exp(jnp.where(causal_in_chunk, diff, 0.0)[None] * log_g[:, None, None]),
                            0.0)
    s = jnp.einsum('bhncd,bhnmd->bhncm', q, k) * intra_decay[None, :, None]
    intra = jnp.einsum('bhncm,bhnme->bhnce', s, v)
    k_w = k * jnp.exp((C - 1 - i)[None, :] * log_g[:, None])[None, :, None, :, None]
    kv = jnp.einsum('bhncd,bhnce->nbhde', k_w, v)
    chunk_decay = jnp.exp(C * log_g)[None, :, None, None]

    def step(state, kv_n):
        return state * chunk_decay + kv_n, state

    _, state_prev = lax.scan(step, jnp.zeros(kv.shape[1:], jnp.float32), kv)
    q_w = q * jnp.exp((i + 1)[None, :] * log_g[:, None])[None, :, None, :, None]
    cross = jnp.einsum('bhncd,nbhde->bhnce', q_w, state_prev)
    return (intra + cross).reshape(B, H, L, dv)


def bidirectional_retention(q, k, v, logit_fwd, logit_bwd):
    lg_f = jax.nn.log_sigmoid(logit_fwd.astype(jnp.float32))
    lg_b = jax.nn.log_sigmoid(logit_bwd.astype(jnp.float32))
    fwd = retention_direction(q, k, v, lg_f)
    bwd = jnp.flip(retention_direction(jnp.flip(q, 2), jnp.flip(k, 2), jnp.flip(v, 2), lg_b), 2)
    return fwd + bwd


def differential_attention(q, k, v, lam):
    B, H, _, L, d = q.shape
    nb = L // Q_BLOCK
    qb = q.reshape(B, H, 2, nb, Q_BLOCK, d).transpose(3, 0, 1, 2, 4, 5)
    scale = d ** -0.5

    def one_block(q_blk):
        s = jnp.einsum('bhtqd,bhtkd->bhtqk', q_blk, k).astype(jnp.float32) * scale
        p = jax.nn.softmax(s, axis=-1)
        a = (p[:, :, 0] - lam * p[:, :, 1]).astype(v.dtype)
        return jnp.einsum('bhqk,bhke->bhqe', a, v)

    o = lax.map(one_block, qb)
    return o.transpose(1, 2, 0, 3, 4).reshape(B, H, L, v.shape[-1])


def setup_inputs(seed: int = 0) -> dict:
    key = jax.random.key(seed)
    ks = jax.random.split(key, 20)

    def dense(k, shape):
        return jax.random.normal(k, shape, jnp.float32) * (shape[-2] ** -0.5)

    def gain(k, shape):
        return 1.0 + 0.02 * jax.random.normal(k, shape, jnp.float32)

    base_logit = jnp.log(2.0 ** (5.0 + jnp.arange(RET_HEADS, dtype=jnp.float32)) - 1.0)
    return {
        "x": jax.random.normal(ks[0], (BATCH, SEQ, D_MODEL), jnp.float32),
        "norm_mix_w": gain(ks[1], (DEPTH, D_MODEL)),
        "w_in": dense(ks[2], (DEPTH, D_MODEL, IN_COLS)),
        "ret_decay_fwd": base_logit[None] + 0.1 * jax.random.normal(ks[3], (DEPTH, RET_HEADS), jnp.float32),
        "ret_decay_bwd": base_logit[None] + 0.1 * jax.random.normal(ks[4], (DEPTH, RET_HEADS), jnp.float32),
        "ret_gn_w": gain(ks[5], (DEPTH, RET_V)),
        "w_ret_out": dense(ks[6], (DEPTH, RET_V, D_MODEL)),
        "q_norm_w": gain(ks[7], (DEPTH, HEAD_DIM)),
        "k_norm_w": gain(ks[8], (DEPTH, HEAD_DIM)),
        "diff_lambda": 0.1 * jax.random.normal(ks[9], (DEPTH, 4, HEAD_DIM), jnp.float32),
        "diff_subln_w": gain(ks[10], (DEPTH, 2 * HEAD_DIM)),
        "w_diff_out": dense(ks[11], (DEPTH, DIFF_V, D_MODEL)),
        "w_out": dense(ks[12], (DEPTH, D_MODEL, D_MODEL)),
        "norm_mlp_w": gain(ks[13], (DEPTH, D_MODEL)),
        "w_mlp_in": dense(ks[14], (DEPTH, D_MODEL, D_FF)),
        "w_mlp_out": dense(ks[15], (DEPTH, D_FF, D_MODEL)),
    }


def reference(x, norm_mix_w, w_in, ret_decay_fwd, ret_decay_bwd, ret_gn_w, w_ret_out,
              q_norm_w, k_norm_w, diff_lambda, diff_subln_w, w_diff_out, w_out,
              norm_mlp_w, w_mlp_in, w_mlp_out):
    B, L, _ = x.shape
    cos, sin = rope_tables(L)
    split_idx = np.cumsum(SPLITS)[:-1].tolist()
    for l in range(DEPTH):
        h = rms_norm(x, norm_mix_w[l])
        proj = jnp.einsum('bld,de->ble', h, w_in[l])
        rq, rk, rv, rg, dq, dk, dv, gr, gd = jnp.split(proj, split_idx, axis=-1)

        rq = apply_rope(rq.reshape(B, L, RET_HEADS, HEAD_DIM).transpose(0, 2, 1, 3), cos, sin)
        rk = apply_rope(rk.reshape(B, L, RET_HEADS, HEAD_DIM).transpose(0, 2, 1, 3), cos, sin) * (HEAD_DIM ** -0.5)
        rv = rv.reshape(B, L, RET_HEADS, RET_V_HEAD_DIM).transpose(0, 2, 1, 3)
        ret = bidirectional_retention(rq, rk, rv, ret_decay_fwd[l], ret_decay_bwd[l])
        ret = head_group_norm(ret.transpose(0, 2, 1, 3),
                              ret_gn_w[l].reshape(RET_HEADS, RET_V_HEAD_DIM)).reshape(B, L, RET_V).astype(x.dtype)
        y_ret = jnp.einsum('ble,ed->bld', jax.nn.silu(rg) * ret, w_ret_out[l])

        dq = dq.reshape(B, L, DIFF_HEADS, 2, HEAD_DIM).transpose(0, 2, 3, 1, 4)
        dk = dk.reshape(B, L, DIFF_HEADS, 2, HEAD_DIM).transpose(0, 2, 3, 1, 4)
        dq = apply_rope(rms_norm(dq, q_norm_w[l]), cos, sin)
        dk = apply_rope(rms_norm(dk, k_norm_w[l]), cos, sin)
        dv = dv.reshape(B, L, DIFF_HEADS, 2 * HEAD_DIM).transpose(0, 2, 1, 3)
        lambda_init = 0.8 - 0.6 * math.exp(-0.3 * l)
        lp = diff_lambda[l].astype(jnp.float32)
        lam = jnp.exp(jnp.sum(lp[0] * lp[1])) - jnp.exp(jnp.sum(lp[2] * lp[3])) + lambda_init
        o = differential_attention(dq, dk, dv, lam)
        o = rms_norm(o, diff_subln_w[l]) * (1.0 - lambda_init)
        y_diff = jnp.einsum('ble,ed->bld', o.transpose(0, 2, 1, 3).reshape(B, L, DIFF_V), w_diff_out[l])

        merged = jax.nn.sigmoid(gr) * y_ret + jax.nn.sigmoid(gd) * y_diff
        x = x + jnp.einsum('bld,de->ble', merged, w_out[l])

        h = rms_norm(x, norm_mlp_w[l])
        u = jnp.square(jax.nn.relu(jnp.einsum('bld,df->blf', h, w_mlp_in[l])))
        x = x + jnp.einsum('blf,fd->bld', u, w_mlp_out[l])
    return x
```

```python
import functools
import math

import jax
import jax.numpy as jnp
from jax import lax
from jax.experimental import pallas as pl
from jax.experimental.pallas import tpu as pltpu

HEAD_DIM = 128
V_HEAD_DIM = 2 * HEAD_DIM
NORM_EPS = 1e-6
GN_EPS = 1e-5
ROPE_THETA = 10000.0
LOG2E = 1.4426950408889634
RET_CHUNK = 256
NEG_BIG = -1e30
VMEM_LIMIT_V7X = 56 * 1024 * 1024

F32 = jnp.float32
BF16 = jnp.bfloat16
_NT = (((1,), (1,)), ((), ()))
_TN = (((0,), (0,)), ((), ()))


def _params(*sem):
    return pltpu.CompilerParams(dimension_semantics=sem, vmem_limit_bytes=VMEM_LIMIT_V7X)


def _rmsnorm_kernel(x_ref, w_ref, o_ref):
    x = x_ref[...]
    ms = jnp.mean(x * x, axis=-1, keepdims=True)
    o_ref[...] = (x * lax.rsqrt(ms + NORM_EPS) * w_ref[...]).astype(o_ref.dtype)


def _rmsnorm(x, w3, layer, tm):
    L, D = x.shape
    tm = min(tm, L)
    return pl.pallas_call(
        _rmsnorm_kernel,
        grid=(L // tm,),
        in_specs=[pl.BlockSpec((tm, D), lambda i: (i, 0)),
                  pl.BlockSpec((None, 1, D), lambda i: (layer, 0, 0))],
        out_specs=pl.BlockSpec((tm, D), lambda i: (i, 0)),
        out_shape=jax.ShapeDtypeStruct((L, D), BF16),
        compiler_params=_params("arbitrary"),
        name="rmsnorm",
    )(x, w3)


def _proj_kernel(a_ref, w_ref, cos_ref, sin_ref, qw_ref, kw_ref, o_ref, wbf_ref, *, tn, d_model):
    j = pl.program_id(0)

    @pl.when(pl.program_id(1) == 0)
    def _cast():
        wbf_ref[...] = w_ref[...].astype(BF16)

    acc = jnp.dot(a_ref[...], wbf_ref[...], preferred_element_type=F32)
    t = d_model // tn
    half = t // 2
    groups = tn // HEAD_DIM

    def rope(y):
        return y * cos_ref[...] + pltpu.roll(y, HEAD_DIM // 2, 1) * sin_ref[...]

    @pl.when(j < t)
    def _ret_qk():
        s = jnp.where(j >= half, HEAD_DIM ** -0.5, 1.0).astype(F32)
        for c in range(groups):
            sl = slice(c * HEAD_DIM, (c + 1) * HEAD_DIM)
            o_ref[:, sl] = (rope(acc[:, sl]) * s).astype(o_ref.dtype)

    @pl.when(((j >= t) & (j < 2 * t)) | ((j >= 5 * t) & (j < 6 * t)))
    def _plain():
        o_ref[...] = acc.astype(o_ref.dtype)

    @pl.when((j >= 2 * t) & (j < 3 * t))
    def _silu():
        o_ref[...] = (acc * jax.nn.sigmoid(acc)).astype(o_ref.dtype)

    @pl.when((j >= 3 * t) & (j < 5 * t))
    def _diff_qk():
        is_q = (j < 4 * t).astype(F32)
        w = is_q * (qw_ref[...] * (HEAD_DIM ** -0.5 * LOG2E)) + (1.0 - is_q) * kw_ref[...]
        for c in range(groups):
            sl = slice(c * HEAD_DIM, (c + 1) * HEAD_DIM)
            y = acc[:, sl]
            ms = jnp.mean(y * y, axis=-1, keepdims=True)
            y = y * lax.rsqrt(ms + NORM_EPS) * w
            o_ref[:, sl] = rope(y).astype(o_ref.dtype)

    @pl.when(j >= 6 * t)
    def _sigmoid():
        o_ref[...] = jax.nn.sigmoid(acc).astype(o_ref.dtype)


def _in_proj(xn, w_in, cos, sin_s, qw3, kw3, layer, tm, tn):
    L, D = xn.shape
    n_cols = w_in.shape[-1]
    tm = min(tm, L)
    tn = min(tn, D // 2)
    return pl.pallas_call(
        functools.partial(_proj_kernel, tn=tn, d_model=D),
        grid=(n_cols // tn, L // tm),
        in_specs=[pl.BlockSpec((tm, D), lambda j, i: (i, 0)),
                  pl.BlockSpec((None, D, tn), lambda j, i: (layer, 0, j)),
                  pl.BlockSpec((tm, HEAD_DIM), lambda j, i: (i, 0)),
                  pl.BlockSpec((tm, HEAD_DIM), lambda j, i: (i, 0)),
                  pl.BlockSpec((None, 1, HEAD_DIM), lambda j, i: (layer, 0, 0)),
                  pl.BlockSpec((None, 1, HEAD_DIM), lambda j, i: (layer, 0, 0))],
        out_specs=pl.BlockSpec((tm, tn), lambda j, i: (i, j)),
        out_shape=jax.ShapeDtypeStruct((L, n_cols), BF16),
        scratch_shapes=[pltpu.VMEM((D, tn), BF16)],
        compiler_params=_params("arbitrary", "arbitrary"),
        name="in_proj",
    )(xn, w_in, cos, sin_s, qw3, kw3)


def _log_sigmoid(x):
    return jnp.minimum(x, 0.0) - jnp.log1p(jnp.exp(-jnp.abs(x)))


def _retention_kernel(lf_ref, lb_ref, q_ref, k_ref, v_ref, g_ref, gnw_ref, o_ref,
                      state_ref, dmask_ref, wq_ref, wk_ref, *, chunk, n_chunks):
    C, N = chunk, n_chunks
    lgf = _log_sigmoid(lf_ref[...])
    lgb = _log_sigmoid(lb_ref[...])
    lgf1, lgb1 = lgf[:, :1], lgb[:, :1]

    row = lax.broadcasted_iota(jnp.int32, (C, HEAD_DIM), 0).astype(F32)
    wq_ref[0] = jnp.exp((row + 1.0) * lgf)
    wq_ref[1] = jnp.exp((C - row) * lgb)
    wk_ref[0] = jnp.exp((C - 1.0 - row) * lgf)
    wk_ref[1] = jnp.exp(row * lgb)
    r = lax.broadcasted_iota(jnp.int32, (C, C), 0)
    c = lax.broadcasted_iota(jnp.int32, (C, C), 1)
    diff = (r - c).astype(F32)
    dmask_ref[...] = jnp.where(diff > 0, jnp.exp(jnp.maximum(diff, 0.0) * lgf1),
                               jnp.where(diff < 0, jnp.exp(jnp.maximum(-diff, 0.0) * lgb1), 2.0))
    dec_f = jnp.exp(C * lgf1)
    dec_b = jnp.exp(C * lgb1)

    def rows(n):
        return pl.ds(pl.multiple_of(n * C, C), C)

    def states(t, carry):
        sf, sb = carry
        nb = N - 1 - t
        kf = (k_ref[rows(t), :].astype(F32) * wk_ref[0]).astype(BF16)
        kb = (k_ref[rows(nb), :].astype(F32) * wk_ref[1]).astype(BF16)
        kvf = lax.dot_general(kf, v_ref[rows(t), :], _TN, preferred_element_type=F32)
        kvb = lax.dot_general(kb, v_ref[rows(nb), :], _TN, preferred_element_type=F32)
        state_ref[t, 0:HEAD_DIM, :] = sf.astype(BF16)
        state_ref[nb, HEAD_DIM:2 * HEAD_DIM, :] = sb.astype(BF16)
        return sf * dec_f + kvf, sb * dec_b + kvb

    zero = jnp.zeros((HEAD_DIM, V_HEAD_DIM), F32)
    lax.fori_loop(0, N, states, (zero, zero))

    def chunk_out(n, carry):
        q = q_ref[rows(n), :]
        s = lax.dot_general(q, k_ref[rows(n), :], _NT, preferred_element_type=F32)
        sd = (s * dmask_ref[...]).astype(BF16)
        qf = q.astype(F32)
        qcat = jnp.concatenate([qf * wq_ref[0], qf * wq_ref[1]], axis=1).astype(BF16)
        o = (jnp.dot(sd, v_ref[rows(n), :], preferred_element_type=F32)
             + jnp.dot(qcat, state_ref[n], preferred_element_type=F32))
        mu = jnp.mean(o, axis=-1, keepdims=True)
        xc = o - mu
        var = jnp.mean(xc * xc, axis=-1, keepdims=True)
        y = xc * lax.rsqrt(var + GN_EPS) * gnw_ref[...]
        o_ref[rows(n), :] = (g_ref[rows(n), :].astype(F32) * y).astype(o_ref.dtype)
        return carry

    lax.fori_loop(0, N, chunk_out, 0)


def _retention(proj, lf4, lb4, gnw4, layer, d_model):
    L = proj.shape[0]
    H = d_model // V_HEAD_DIM
    C = min(RET_CHUNK, L)
    N = L // C
    qk_blocks = d_model // 2 // HEAD_DIM
    v_blocks = d_model // V_HEAD_DIM
    return pl.pallas_call(
        functools.partial(_retention_kernel, chunk=C, n_chunks=N),
        grid=(H,),
        in_specs=[pl.BlockSpec((None, None, 1, HEAD_DIM), lambda h: (layer, h, 0, 0)),
                  pl.BlockSpec((None, None, 1, HEAD_DIM), lambda h: (layer, h, 0, 0)),
                  pl.BlockSpec((L, HEAD_DIM), lambda h: (0, h)),
                  pl.BlockSpec((L, HEAD_DIM), lambda h: (0, qk_blocks + h)),
                  pl.BlockSpec((L, V_HEAD_DIM), lambda h: (0, v_blocks + h)),
                  pl.BlockSpec((L, V_HEAD_DIM), lambda h: (0, 2 * v_blocks + h)),
                  pl.BlockSpec((None, None, 1, V_HEAD_DIM), lambda h: (layer, h, 0, 0))],
        out_specs=pl.BlockSpec((L, V_HEAD_DIM), lambda h: (0, h)),
        out_shape=jax.ShapeDtypeStruct((L, d_model), BF16),
        scratch_shapes=[pltpu.VMEM((N, 2 * HEAD_DIM, V_HEAD_DIM), BF16),
                        pltpu.VMEM((C, C), F32),
                        pltpu.VMEM((2, C, HEAD_DIM), F32),
                        pltpu.VMEM((2, C, HEAD_DIM), F32)],
        compiler_params=_params("arbitrary"),
        name="retention",
    )(lf4, lb4, proj, proj, proj, proj, gnw4)


def _attn_kernel(lam_ref, q_ref, k_ref, v_ref, w_ref, o_ref, acc_ref, *, tk, n_kv, lambda_init):
    tq = q_ref.shape[0]
    q = q_ref[...]
    qs = (q[:, :HEAD_DIM], q[:, HEAD_DIM:])
    acc_ref[...] = jnp.zeros_like(acc_ref)

    def body(j, carry):
        rows = pl.ds(pl.multiple_of(j * tk, tk), tk)
        k = k_ref[rows, :]
        v = v_ref[rows, :]
        out = []
        for t in range(2):
            m, l = carry[2 * t], carry[2 * t + 1]
            s = lax.dot_general(qs[t], k[:, t * HEAD_DIM:(t + 1) * HEAD_DIM], _NT,
                                preferred_element_type=F32)
            m_new = jnp.maximum(m, jnp.max(s, axis=-1, keepdims=True))
            alpha = jnp.exp2(m - m_new)
            p = jnp.exp2(s - m_new)
            l_new = alpha * l + jnp.sum(p, axis=-1, keepdims=True)
            acc_ref[t] = alpha * acc_ref[t] + jnp.dot(p.astype(BF16), v, preferred_element_type=F32)
            out += [m_new, l_new]
        return tuple(out)

    m0 = jnp.full((tq, 1), NEG_BIG, F32)
    l0 = jnp.zeros((tq, 1), F32)
    _, l1, _, l2 = lax.fori_loop(0, n_kv, body, (m0, l0, m0, l0))

    lp = lam_ref[...]
    lam = (jnp.exp(jnp.sum(lp[0:1] * lp[1:2], axis=-1, keepdims=True))
           - jnp.exp(jnp.sum(lp[2:3] * lp[3:4], axis=-1, keepdims=True)) + lambda_init)
    o = acc_ref[0] * (1.0 / l1) - lam * (acc_ref[1] * (1.0 / l2))
    ms = jnp.mean(o * o, axis=-1, keepdims=True)
    y = o * lax.rsqrt(ms + NORM_EPS) * w_ref[...] * (1.0 - lambda_init)
    o_ref[...] = y.astype(o_ref.dtype)


def _diff_attention(proj, diff_lambda, subln3, layer, d_model, tq, tk):
    L = proj.shape[0]
    H = d_model // V_HEAD_DIM
    tq = min(tq, L)
    tk = min(tk, L)
    vb = d_model // V_HEAD_DIM
    lambda_init = 0.8 - 0.6 * math.exp(-0.3 * layer)
    return pl.pallas_call(
        functools.partial(_attn_kernel, tk=tk, n_kv=L // tk, lambda_init=lambda_init),
        grid=(H, L // tq),
        in_specs=[pl.BlockSpec((None, 4, HEAD_DIM), lambda h, i: (layer, 0, 0)),
                  pl.BlockSpec((tq, V_HEAD_DIM), lambda h, i: (i, 3 * vb + h)),
                  pl.BlockSpec((L, V_HEAD_DIM), lambda h, i: (0, 4 * vb + h)),
                  pl.BlockSpec((L, V_HEAD_DIM), lambda h, i: (0, 5 * vb + h)),
                  pl.BlockSpec((None, 1, V_HEAD_DIM), lambda h, i: (layer, 0, 0))],
        out_specs=pl.BlockSpec((tq, V_HEAD_DIM), lambda h, i: (i, h)),
        out_shape=jax.ShapeDtypeStruct((L, d_model), BF16),
        scratch_shapes=[pltpu.VMEM((2, tq, V_HEAD_DIM), F32)],
        compiler_params=_params("arbitrary", "arbitrary"),
        name="diff_attention",
    )(diff_lambda, proj, proj, proj, subln3)


def _merge_kernel(a1_ref, a2_ref, w1_ref, w2_ref, g1_ref, g2_ref, o_ref, w1bf_ref, w2bf_ref):
    @pl.when(pl.program_id(1) == 0)
    def _cast():
        w1bf_ref[...] = w1_ref[...].astype(BF16)
        w2bf_ref[...] = w2_ref[...].astype(BF16)

    y1 = jnp.dot(a1_ref[...], w1bf_ref[...], preferred_element_type=F32)
    y2 = jnp.dot(a2_ref[...], w2bf_ref[...], preferred_element_type=F32)
    o_ref[...] = (g1_ref[...].astype(F32) * y1 + g2_ref[...].astype(F32) * y2).astype(o_ref.dtype)


def _merge(ret, att, w_ret_out, w_diff_out, proj, layer, tm, tn):
    L, D = ret.shape
    tm = min(tm, L)
    tn = min(tn, D)
    gb = D // tn
    return pl.pallas_call(
        _merge_kernel,
        grid=(D // tn, L // tm),
        in_specs=[pl.BlockSpec((tm, D), lambda j, i: (i, 0)),
                  pl.BlockSpec((tm, D), lambda j, i: (i, 0)),
                  pl.BlockSpec((None, D, tn), lambda j, i: (layer, 0, j)),
                  pl.BlockSpec((None, D, tn), lambda j, i: (layer, 0, j)),
                  pl.BlockSpec((tm, tn), lambda j, i: (i, 6 * gb + j)),
                  pl.BlockSpec((tm, tn), lambda j, i: (i, 7 * gb + j))],
        out_specs=pl.BlockSpec((tm, tn), lambda j, i: (i, j)),
        out_shape=jax.ShapeDtypeStruct((L, D), BF16),
        scratch_shapes=[pltpu.VMEM((D, tn), BF16), pltpu.VMEM((D, tn), BF16)],
        compiler_params=_params("arbitrary", "arbitrary"),
        name="merge",
    )(ret, att, w_ret_out, w_diff_out, proj, proj)


def _matmul_kernel(a_ref, w_ref, *rest, residual, relu2):
    if residual:
        x_ref, o_ref, wbf_ref = rest
    else:
        o_ref, wbf_ref = rest

    @pl.when(pl.program_id(1) == 0)
    def _cast():
        wbf_ref[...] = w_ref[...].astype(BF16)

    y = jnp.dot(a_ref[...], wbf_ref[...], preferred_element_type=F32)
    if relu2:
        y = jnp.square(jnp.maximum(y, 0.0))
    if residual:
        y = x_ref[...] + y
    o_ref[...] = y.astype(o_ref.dtype)


def _matmul(a, w, layer, tm, tn, *, x=None, relu2=False, out_dtype=F32, name):
    L, K = a.shape
    n_cols = w.shape[-1]
    tm = min(tm, L)
    tn = min(tn, n_cols)
    in_specs = [pl.BlockSpec((tm, K), lambda j, i: (i, 0)),
                pl.BlockSpec((None, K, tn), lambda j, i: (layer, 0, j))]
    args = [a, w]
    if x is not None:
        in_specs.append(pl.BlockSpec((tm, tn), lambda j, i: (i, j)))
        args.append(x)
    return pl.pallas_call(
        functools.partial(_matmul_kernel, residual=x is not None, relu2=relu2),
        grid=(n_cols // tn, L // tm),
        in_specs=in_specs,
        out_specs=pl.BlockSpec((tm, tn), lambda j, i: (i, j)),
        out_shape=jax.ShapeDtypeStruct((L, n_cols), out_dtype),
        scratch_shapes=[pltpu.VMEM((K, tn), BF16)],
        compiler_params=_params("arbitrary", "arbitrary"),
        name=name,
    )(*args)


def _rope_tables(L):
    inv = 1.0 / (ROPE_THETA ** (jnp.arange(0, HEAD_DIM, 2, dtype=F32) / HEAD_DIM))
    ang = jnp.arange(L, dtype=F32)[:, None] * inv[None, :]
    ang = jnp.concatenate([ang, ang], axis=-1)
    sign = jnp.where(jnp.arange(HEAD_DIM) < HEAD_DIM // 2, -1.0, 1.0).astype(F32)
    return jnp.cos(ang), jnp.sin(ang) * sign[None, :]


def kernel(x, norm_mix_w, w_in, ret_decay_fwd, ret_decay_bwd, ret_gn_w, w_ret_out, q_norm_w, k_norm_w,
           diff_lambda, diff_subln_w, w_diff_out, w_out, norm_mlp_w, w_mlp_in, w_mlp_out):
    B, L, D = x.shape
    assert B == 1 and D % V_HEAD_DIM == 0
    depth = w_in.shape[0]
    H = D // V_HEAD_DIM
    cos, sin_s = _rope_tables(L)

    norm_mix3 = norm_mix_w.reshape(depth, 1, D)
    norm_mlp3 = norm_mlp_w.reshape(depth, 1, D)
    qw3 = q_norm_w.reshape(depth, 1, HEAD_DIM)
    kw3 = k_norm_w.reshape(depth, 1, HEAD_DIM)
    subln3 = diff_subln_w.reshape(depth, 1, V_HEAD_DIM)
    gnw4 = ret_gn_w.reshape(depth, H, 1, V_HEAD_DIM)
    lf4 = jnp.broadcast_to(ret_decay_fwd[:, :, None, None], (depth, H, 1, HEAD_DIM))
    lb4 = jnp.broadcast_to(ret_decay_bwd[:, :, None, None], (depth, H, 1, HEAD_DIM))

    h = x.reshape(L, D)
    for layer in range(depth):
        xn = _rmsnorm(h, norm_mix3, layer, 512)
        proj = _in_proj(xn, w_in, cos, sin_s, qw3, kw3, layer, 1024, 1024)
        ret = _retention(proj, lf4, lb4, gnw4, layer, D)
        att = _diff_attention(proj, diff_lambda, subln3, layer, D, 512, 1024)
        merged = _merge(ret, att, w_ret_out, w_diff_out, proj, layer, 512, 512)
        h = _matmul(merged, w_out, layer, 1024, 512, x=h, name="out_proj")
        hn = _rmsnorm(h, norm_mlp3, layer, 512)
        u = _matmul(hn, w_mlp_in, layer, 1024, 1024, relu2=True, out_dtype=BF16, name="mlp_in")
        h = _matmul(u, w_mlp_out, layer, 512, 256, x=h, name="mlp_out")
    return h.reshape(B, L, D)
```

```python
import functools
import math

import jax
import jax.numpy as jnp
from jax import lax
from jax.experimental import pallas as pl
from jax.experimental.pallas import tpu as pltpu

HEAD_DIM = 128
V_HEAD_DIM = 2 * HEAD_DIM
NORM_EPS = 1e-6
GN_EPS = 1e-5
ROPE_THETA = 10000.0
LOG2E = 1.4426950408889634
RET_CHUNK = 256
NEG_BIG = -1e30
MAX_STATIC_SHIFT = 50.0
SHIFT_MARGIN = 1.01
VMEM_LIMIT_V7X = 56 * 1024 * 1024

F32 = jnp.float32
BF16 = jnp.bfloat16
_NT = (((1,), (1,)), ((), ()))
_TN = (((0,), (0,)), ((), ()))


def _params(*sem):
    return pltpu.CompilerParams(dimension_semantics=sem, vmem_limit_bytes=VMEM_LIMIT_V7X)


def _rmsnorm_kernel(x_ref, w_ref, o_ref):
    x = x_ref[...]
    ms = jnp.mean(x * x, axis=-1, keepdims=True)
    o_ref[...] = (x * lax.rsqrt(ms + NORM_EPS) * w_ref[...]).astype(o_ref.dtype)


def _rmsnorm(x, w3, layer, tm):
    L, D = x.shape
    tm = min(tm, L)
    return pl.pallas_call(
        _rmsnorm_kernel,
        grid=(L // tm,),
        in_specs=[pl.BlockSpec((tm, D), lambda i: (i, 0)),
                  pl.BlockSpec((None, 1, D), lambda i: (layer, 0, 0))],
        out_specs=pl.BlockSpec((tm, D), lambda i: (i, 0)),
        out_shape=jax.ShapeDtypeStruct((L, D), BF16),
        compiler_params=_params("arbitrary"),
        name="rmsnorm",
    )(x, w3)


def _proj_kernel(a_ref, w_ref, cos_ref, sin_ref, qw_ref, kw_ref, o_ref, wbf_ref, *, tn, d_model):
    j = pl.program_id(0)

    @pl.when(pl.program_id(1) == 0)
    def _cast():
        wbf_ref[...] = w_ref[...].astype(BF16)

    acc = jnp.dot(a_ref[...], wbf_ref[...], preferred_element_type=F32)
    t = d_model // tn
    half = t // 2
    groups = tn // HEAD_DIM

    def rope(y):
        return y * cos_ref[...] + pltpu.roll(y, HEAD_DIM // 2, 1) * sin_ref[...]

    @pl.when(j < t)
    def _ret_qk():
        s = jnp.where(j >= half, HEAD_DIM ** -0.5, 1.0).astype(F32)
        for c in range(groups):
            sl = slice(c * HEAD_DIM, (c + 1) * HEAD_DIM)
            o_ref[:, sl] = (rope(acc[:, sl]) * s).astype(o_ref.dtype)

    @pl.when(((j >= t) & (j < 2 * t)) | ((j >= 5 * t) & (j < 6 * t)))
    def _plain():
        o_ref[...] = acc.astype(o_ref.dtype)

    @pl.when((j >= 2 * t) & (j < 3 * t))
    def _silu():
        o_ref[...] = (acc * jax.nn.sigmoid(acc)).astype(o_ref.dtype)

    @pl.when((j >= 3 * t) & (j < 5 * t))
    def _diff_qk():
        is_q = (j < 4 * t).astype(F32)
        w = is_q * (qw_ref[...] * (HEAD_DIM ** -0.5 * LOG2E)) + (1.0 - is_q) * kw_ref[...]
        for c in range(groups):
            sl = slice(c * HEAD_DIM, (c + 1) * HEAD_DIM)
            y = acc[:, sl]
            ms = jnp.mean(y * y, axis=-1, keepdims=True)
            y = y * lax.rsqrt(ms + NORM_EPS) * w
            o_ref[:, sl] = rope(y).astype(o_ref.dtype)

    @pl.when(j >= 6 * t)
    def _sigmoid():
        o_ref[...] = jax.nn.sigmoid(acc).astype(o_ref.dtype)


def _in_proj(xn, w_in, cos, sin_s, qw3, kw3, layer, tm, tn):
    L, D = xn.shape
    n_cols = w_in.shape[-1]
    tm = min(tm, L)
    tn = min(tn, D // 2)
    return pl.pallas_call(
        functools.partial(_proj_kernel, tn=tn, d_model=D),
        grid=(n_cols // tn, L // tm),
        in_specs=[pl.BlockSpec((tm, D), lambda j, i: (i, 0)),
                  pl.BlockSpec((None, D, tn), lambda j, i: (layer, 0, j)),
                  pl.BlockSpec((tm, HEAD_DIM), lambda j, i: (i, 0)),
                  pl.BlockSpec((tm, HEAD_DIM), lambda j, i: (i, 0)),
                  pl.BlockSpec((None, 1, HEAD_DIM), lambda j, i: (layer, 0, 0)),
                  pl.BlockSpec((None, 1, HEAD_DIM), lambda j, i: (layer, 0, 0))],
        out_specs=pl.BlockSpec((tm, tn), lambda j, i: (i, j)),
        out_shape=jax.ShapeDtypeStruct((L, n_cols), BF16),
        scratch_shapes=[pltpu.VMEM((D, tn), BF16)],
        compiler_params=_params("arbitrary", "arbitrary"),
        name="in_proj",
    )(xn, w_in, cos, sin_s, qw3, kw3)


def _log_sigmoid(x):
    return jnp.minimum(x, 0.0) - jnp.log1p(jnp.exp(-jnp.abs(x)))


def _retention_kernel(lf_ref, lb_ref, q_ref, k_ref, v_ref, g_ref, gnw_ref, o_ref,
                      state_ref, dmask_ref, wq_ref, wk_ref, *, chunk, n_chunks):
    C, N = chunk, n_chunks
    lgf = _log_sigmoid(lf_ref[...])
    lgb = _log_sigmoid(lb_ref[...])
    lgf1, lgb1 = lgf[:, :1], lgb[:, :1]

    row = lax.broadcasted_iota(jnp.int32, (C, HEAD_DIM), 0).astype(F32)
    wq_ref[0] = jnp.exp((row + 1.0) * lgf)
    wq_ref[1] = jnp.exp((C - row) * lgb)
    wk_ref[0] = jnp.exp((C - 1.0 - row) * lgf)
    wk_ref[1] = jnp.exp(row * lgb)
    r = lax.broadcasted_iota(jnp.int32, (C, C), 0)
    c = lax.broadcasted_iota(jnp.int32, (C, C), 1)
    diff = (r - c).astype(F32)
    dmask_ref[...] = jnp.where(diff > 0, jnp.exp(jnp.maximum(diff, 0.0) * lgf1),
                               jnp.where(diff < 0, jnp.exp(jnp.maximum(-diff, 0.0) * lgb1), 2.0))
    dec_f = jnp.exp(C * lgf1)
    dec_b = jnp.exp(C * lgb1)

    def rows(n):
        return pl.ds(pl.multiple_of(n * C, C), C)

    def states(t, carry):
        sf, sb = carry
        nb = N - 1 - t
        kf = (k_ref[rows(t), :].astype(F32) * wk_ref[0]).astype(BF16)
        kb = (k_ref[rows(nb), :].astype(F32) * wk_ref[1]).astype(BF16)
        kvf = lax.dot_general(kf, v_ref[rows(t), :], _TN, preferred_element_type=F32)
        kvb = lax.dot_general(kb, v_ref[rows(nb), :], _TN, preferred_element_type=F32)
        state_ref[t, 0:HEAD_DIM, :] = sf.astype(BF16)
        state_ref[nb, HEAD_DIM:2 * HEAD_DIM, :] = sb.astype(BF16)
        return sf * dec_f + kvf, sb * dec_b + kvb

    zero = jnp.zeros((HEAD_DIM, V_HEAD_DIM), F32)
    lax.fori_loop(0, N, states, (zero, zero))

    def chunk_out(n, carry):
        q = q_ref[rows(n), :]
        s = lax.dot_general(q, k_ref[rows(n), :], _NT, preferred_element_type=F32)
        sd = (s * dmask_ref[...]).astype(BF16)
        qf = q.astype(F32)
        qcat = jnp.concatenate([qf * wq_ref[0], qf * wq_ref[1]], axis=1).astype(BF16)
        o = (jnp.dot(sd, v_ref[rows(n), :], preferred_element_type=F32)
             + jnp.dot(qcat, state_ref[n], preferred_element_type=F32))
        mu = jnp.mean(o, axis=-1, keepdims=True)
        xc = o - mu
        var = jnp.mean(xc * xc, axis=-1, keepdims=True)
        y = xc * lax.rsqrt(var + GN_EPS) * gnw_ref[...]
        o_ref[rows(n), :] = (g_ref[rows(n), :].astype(F32) * y).astype(o_ref.dtype)
        return carry

    lax.fori_loop(0, N, chunk_out, 0)


def _retention(proj, lf4, lb4, gnw4, layer, d_model):
    L = proj.shape[0]
    H = d_model // V_HEAD_DIM
    C = min(RET_CHUNK, L)
    N = L // C
    qk_blocks = d_model // 2 // HEAD_DIM
    v_blocks = d_model // V_HEAD_DIM
    return pl.pallas_call(
        functools.partial(_retention_kernel, chunk=C, n_chunks=N),
        grid=(H,),
        in_specs=[pl.BlockSpec((None, None, 1, HEAD_DIM), lambda h: (layer, h, 0, 0)),
                  pl.BlockSpec((None, None, 1, HEAD_DIM), lambda h: (layer, h, 0, 0)),
                  pl.BlockSpec((L, HEAD_DIM), lambda h: (0, h)),
                  pl.BlockSpec((L, HEAD_DIM), lambda h: (0, qk_blocks + h)),
                  pl.BlockSpec((L, V_HEAD_DIM), lambda h: (0, v_blocks + h)),
                  pl.BlockSpec((L, V_HEAD_DIM), lambda h: (0, 2 * v_blocks + h)),
                  pl.BlockSpec((None, None, 1, V_HEAD_DIM), lambda h: (layer, h, 0, 0))],
        out_specs=pl.BlockSpec((L, V_HEAD_DIM), lambda h: (0, h)),
        out_shape=jax.ShapeDtypeStruct((L, d_model), BF16),
        scratch_shapes=[pltpu.VMEM((N, 2 * HEAD_DIM, V_HEAD_DIM), BF16),
                        pltpu.VMEM((C, C), F32),
                        pltpu.VMEM((2, C, HEAD_DIM), F32),
                        pltpu.VMEM((2, C, HEAD_DIM), F32)],
        compiler_params=_params("arbitrary"),
        name="retention",
    )(lf4, lb4, proj, proj, proj, proj, gnw4)


def _attn_kernel(lam_ref, q_ref, k_ref, v_ref, w_ref, o_ref, acc_ref, l_ref, shift_ref, knorm_ref,
                 *, tk, n_kv, lambda_init):
    tq = q_ref.shape[0]
    L = k_ref.shape[0]
    q = q_ref[...]
    qs = (q[:, :HEAD_DIM], q[:, HEAD_DIM:])

    @pl.when(pl.program_id(1) == 0)
    def _key_norms():
        rc = min(L, 512)

        def kbody(c, carry):
            kf = k_ref[pl.ds(pl.multiple_of(c * rc, rc), rc), :].astype(F32)
            kk = kf * kf
            return tuple(jnp.maximum(carry[t], jnp.sum(kk[:, t * HEAD_DIM:(t + 1) * HEAD_DIM],
                                                        axis=-1, keepdims=True)) for t in range(2))

        z = jnp.zeros((rc, 1), F32)
        k2 = lax.fori_loop(0, L // rc, kbody, (z, z))
        for t in range(2):
            kn = jnp.sqrt(jnp.max(k2[t], axis=0, keepdims=True))
            knorm_ref[t] = jnp.broadcast_to(kn, (1, HEAD_DIM))

    qf = q.astype(F32)
    bound = None
    for t in range(2):
        qt = qf[:, t * HEAD_DIM:(t + 1) * HEAD_DIM]
        qn = jnp.sqrt(jnp.sum(qt * qt, axis=-1, keepdims=True))
        shift = qn * knorm_ref[t][:, :1] * SHIFT_MARGIN
        shift_ref[t] = jnp.broadcast_to(shift, (tq, HEAD_DIM))
        bound = jnp.max(shift) if bound is None else jnp.maximum(bound, jnp.max(shift))

    def kv_rows(j):
        return pl.ds(pl.multiple_of(j * tk, tk), tk)

    def finish(l1, l2):
        lp = lam_ref[...]
        lam = (jnp.exp(jnp.sum(lp[0:1] * lp[1:2], axis=-1, keepdims=True))
               - jnp.exp(jnp.sum(lp[2:3] * lp[3:4], axis=-1, keepdims=True)) + lambda_init)
        o = acc_ref[0] * (1.0 / l1) - lam * (acc_ref[1] * (1.0 / l2))
        ms = jnp.mean(o * o, axis=-1, keepdims=True)
        y = o * lax.rsqrt(ms + NORM_EPS) * w_ref[...] * (1.0 - lambda_init)
        o_ref[...] = y.astype(o_ref.dtype)

    @pl.when(bound <= MAX_STATIC_SHIFT)
    def _static_shift():
        acc_ref[...] = jnp.zeros_like(acc_ref)
        l_ref[...] = jnp.zeros_like(l_ref)

        def body(j, carry):
            k = k_ref[kv_rows(j), :]
            v = v_ref[kv_rows(j), :]
            for t in range(2):
                s = lax.dot_general(qs[t], k[:, t * HEAD_DIM:(t + 1) * HEAD_DIM], _NT,
                                    preferred_element_type=F32)
                shift = shift_ref[t]
                lsum = l_ref[t]
                ps = []
                for c in range(tk // HEAD_DIM):
                    pc = jnp.exp2(s[:, c * HEAD_DIM:(c + 1) * HEAD_DIM] - shift)
                    lsum = lsum + pc
                    ps.append(pc.astype(BF16))
                l_ref[t] = lsum
                acc_ref[t] += jnp.dot(jnp.concatenate(ps, axis=1), v, preferred_element_type=F32)
            return carry

        lax.fori_loop(0, n_kv, body, 0)
        finish(jnp.sum(l_ref[0], axis=-1, keepdims=True), jnp.sum(l_ref[1], axis=-1, keepdims=True))

    @pl.when(bound > MAX_STATIC_SHIFT)
    def _online():
        acc_ref[...] = jnp.zeros_like(acc_ref)

        def body(j, carry):
            k = k_ref[kv_rows(j), :]
            v = v_ref[kv_rows(j), :]
            out = []
            for t in range(2):
                m, l = carry[2 * t], carry[2 * t + 1]
                s = lax.dot_general(qs[t], k[:, t * HEAD_DIM:(t + 1) * HEAD_DIM], _NT,
                                    preferred_element_type=F32)
                m_new = jnp.maximum(m, jnp.max(s, axis=-1, keepdims=True))
                alpha = jnp.exp2(m - m_new)
                p = jnp.exp2(s - m_new)
                l_new = alpha * l + jnp.sum(p, axis=-1, keepdims=True)
                acc_ref[t] = alpha * acc_ref[t] + jnp.dot(p.astype(BF16), v, preferred_element_type=F32)
                out += [m_new, l_new]
            return tuple(out)

        m0 = jnp.full((tq, 1), NEG_BIG, F32)
        l0 = jnp.zeros((tq, 1), F32)
        _, l1, _, l2 = lax.fori_loop(0, n_kv, body, (m0, l0, m0, l0))
        finish(l1, l2)


def _diff_attention(proj, diff_lambda, subln3, layer, d_model, tq, tk):
    L = proj.shape[0]
    H = d_model // V_HEAD_DIM
    tq = min(tq, L)
    tk = min(tk, L)
    vb = d_model // V_HEAD_DIM
    lambda_init = 0.8 - 0.6 * math.exp(-0.3 * layer)
    return pl.pallas_call(
        functools.partial(_attn_kernel, tk=tk, n_kv=L // tk, lambda_init=lambda_init),
        grid=(H, L // tq),
        in_specs=[pl.BlockSpec((None, 4, HEAD_DIM), lambda h, i: (layer, 0, 0)),
                  pl.BlockSpec((tq, V_HEAD_DIM), lambda h, i: (i, 3 * vb + h)),
                  pl.BlockSpec((L, V_HEAD_DIM), lambda h, i: (0, 4 * vb + h)),
                  pl.BlockSpec((L, V_HEAD_DIM), lambda h, i: (0, 5 * vb + h)),
                  pl.BlockSpec((None, 1, V_HEAD_DIM), lambda h, i: (layer, 0, 0))],
        out_specs=pl.BlockSpec((tq, V_HEAD_DIM), lambda h, i: (i, h)),
        out_shape=jax.ShapeDtypeStruct((L, d_model), BF16),
        scratch_shapes=[pltpu.VMEM((2, tq, V_HEAD_DIM), F32),
                        pltpu.VMEM((2, tq, HEAD_DIM), F32),
                        pltpu.VMEM((2, tq, HEAD_DIM), F32),
                        pltpu.VMEM((2, 1, HEAD_DIM), F32)],
        compiler_params=_params("arbitrary", "arbitrary"),
        name="diff_attention",
    )(diff_lambda, proj, proj, proj, subln3)


def _merge_kernel(a1_ref, a2_ref, w1_ref, w2_ref, g1_ref, g2_ref, o_ref, w1bf_ref, w2bf_ref):
    @pl.when(pl.program_id(1) == 0)
    def _cast():
        w1bf_ref[...] = w1_ref[...].astype(BF16)
        w2bf_ref[...] = w2_ref[...].astype(BF16)

    y1 = jnp.dot(a1_ref[...], w1bf_ref[...], preferred_element_type=F32)
    y2 = jnp.dot(a2_ref[...], w2bf_ref[...], preferred_element_type=F32)
    o_ref[...] = (g1_ref[...].astype(F32) * y1 + g2_ref[...].astype(F32) * y2).astype(o_ref.dtype)


def _merge(ret, att, w_ret_out, w_diff_out, proj, layer, tm, tn):
    L, D = ret.shape
    tm = min(tm, L)
    tn = min(tn, D)
    gb = D // tn
    return pl.pallas_call(
        _merge_kernel,
        grid=(D // tn, L // tm),
        in_specs=[pl.BlockSpec((tm, D), lambda j, i: (i, 0)),
                  pl.BlockSpec((tm, D), lambda j, i: (i, 0)),
                  pl.BlockSpec((None, D, tn), lambda j, i: (layer, 0, j)),
                  pl.BlockSpec((None, D, tn), lambda j, i: (layer, 0, j)),
                  pl.BlockSpec((tm, tn), lambda j, i: (i, 6 * gb + j)),
                  pl.BlockSpec((tm, tn), lambda j, i: (i, 7 * gb + j))],
        out_specs=pl.BlockSpec((tm, tn), lambda j, i: (i, j)),
        out_shape=jax.ShapeDtypeStruct((L, D), BF16),
        scratch_shapes=[pltpu.VMEM((D, tn), BF16), pltpu.VMEM((D, tn), BF16)],
        compiler_params=_params("arbitrary", "arbitrary"),
        name="merge",
    )(ret, att, w_ret_out, w_diff_out, proj, proj)


def _matmul_kernel(a_ref, w_ref, *rest, residual, relu2):
    if residual:
        x_ref, o_ref, wbf_ref = rest
    else:
        o_ref, wbf_ref = rest

    @pl.when(pl.program_id(1) == 0)
    def _cast():
        wbf_ref[...] = w_ref[...].astype(BF16)

    y = jnp.dot(a_ref[...], wbf_ref[...], preferred_element_type=F32)
    if relu2:
        y = jnp.square(jnp.maximum(y, 0.0))
    if residual:
        y = x_ref[...] + y
    o_ref[...] = y.astype(o_ref.dtype)


def _matmul(a, w, layer, tm, tn, *, x=None, relu2=False, out_dtype=F32, name):
    L, K = a.shape
    n_cols = w.shape[-1]
    tm = min(tm, L)
    tn = min(tn, n_cols)
    in_specs = [pl.BlockSpec((tm, K), lambda j, i: (i, 0)),
                pl.BlockSpec((None, K, tn), lambda j, i: (layer, 0, j))]
    args = [a, w]
    if x is not None:
        in_specs.append(pl.BlockSpec((tm, tn), lambda j, i: (i, j)))
        args.append(x)
    return pl.pallas_call(
        functools.partial(_matmul_kernel, residual=x is not None, relu2=relu2),
        grid=(n_cols // tn, L // tm),
        in_specs=in_specs,
        out_specs=pl.BlockSpec((tm, tn), lambda j, i: (i, j)),
        out_shape=jax.ShapeDtypeStruct((L, n_cols), out_dtype),
        scratch_shapes=[pltpu.VMEM((K, tn), BF16)],
        compiler_params=_params("arbitrary", "arbitrary"),
        name=name,
    )(*args)


def _rope_tables(L):
    inv = 1.0 / (ROPE_THETA ** (jnp.arange(0, HEAD_DIM, 2, dtype=F32) / HEAD_DIM))
    ang = jnp.arange(L, dtype=F32)[:, None] * inv[None, :]
    ang = jnp.concatenate([ang, ang], axis=-1)
    sign = jnp.where(jnp.arange(HEAD_DIM) < HEAD_DIM // 2, -1.0, 1.0).astype(F32)
    return jnp.cos(ang), jnp.sin(ang) * sign[None, :]


def kernel(x, norm_mix_w, w_in, ret_decay_fwd, ret_decay_bwd, ret_gn_w, w_ret_out, q_norm_w, k_norm_w,
           diff_lambda, diff_subln_w, w_diff_out, w_out, norm_mlp_w, w_mlp_in, w_mlp_out):
    B, L, D = x.shape
    assert B == 1 and D % V_HEAD_DIM == 0
    depth = w_in.shape[0]
    H = D // V_HEAD_DIM
    cos, sin_s = _rope_tables(L)

    norm_mix3 = norm_mix_w.reshape(depth, 1, D)
    norm_mlp3 = norm_mlp_w.reshape(depth, 1, D)
    qw3 = q_norm_w.reshape(depth, 1, HEAD_DIM)
    kw3 = k_norm_w.reshape(depth, 1, HEAD_DIM)
    subln3 = diff_subln_w.reshape(depth, 1, V_HEAD_DIM)
    gnw4 = ret_gn_w.reshape(depth, H, 1, V_HEAD_DIM)
    lf4 = jnp.broadcast_to(ret_decay_fwd[:, :, None, None], (depth, H, 1, HEAD_DIM))
    lb4 = jnp.broadcast_to(ret_decay_bwd[:, :, None, None], (depth, H, 1, HEAD_DIM))

    h = x.reshape(L, D)
    for layer in range(depth):
        xn = _rmsnorm(h, norm_mix3, layer, 512)
        proj = _in_proj(xn, w_in, cos, sin_s, qw3, kw3, layer, 1024, 1024)
        ret = _retention(proj, lf4, lb4, gnw4, layer, D)
        att = _diff_attention(proj, diff_lambda, subln3, layer, D, 512, 1024)
        merged = _merge(ret, att, w_ret_out, w_diff_out, proj, layer, 512, 512)
        h = _matmul(merged, w_out, layer, 1024, 512, x=h, name="out_proj")
        hn = _rmsnorm(h, norm_mlp3, layer, 512)
        u = _matmul(hn, w_mlp_in, layer, 1024, 1024, relu2=True, out_dtype=BF16, name="mlp_in")
        h = _matmul(u, w_mlp_out, layer, 512, 256, x=h, name="mlp_out")
    return h.reshape(B, L, D)
```

```python
import functools
import math

import jax
import jax.numpy as jnp
from jax import lax
from jax.experimental import pallas as pl
from jax.experimental.pallas import tpu as pltpu

HEAD_DIM = 128
V_HEAD_DIM = 2 * HEAD_DIM
NORM_EPS = 1e-6
GN_EPS = 1e-5
ROPE_THETA = 10000.0
LOG2E = 1.4426950408889634
RET_CHUNK = 256
RET_UNROLL = 4
NEG_BIG = -1e30
MAX_STATIC_SHIFT = 50.0
SHIFT_MARGIN = 1.01
VMEM_LIMIT_V7X = 56 * 1024 * 1024

F32 = jnp.float32
BF16 = jnp.bfloat16
_NT = (((1,), (1,)), ((), ()))
_TN = (((0,), (0,)), ((), ()))


def _params(*sem):
    return pltpu.CompilerParams(dimension_semantics=sem, vmem_limit_bytes=VMEM_LIMIT_V7X)


def _rmsnorm_kernel(x_ref, w_ref, o_ref):
    x = x_ref[...]
    ms = jnp.mean(x * x, axis=-1, keepdims=True)
    o_ref[...] = (x * lax.rsqrt(ms + NORM_EPS) * w_ref[...]).astype(o_ref.dtype)


def _rmsnorm(x, w3, layer, tm):
    L, D = x.shape
    tm = min(tm, L)
    return pl.pallas_call(
        _rmsnorm_kernel,
        grid=(L // tm,),
        in_specs=[pl.BlockSpec((tm, D), lambda i: (i, 0)),
                  pl.BlockSpec((None, 1, D), lambda i: (layer, 0, 0))],
        out_specs=pl.BlockSpec((tm, D), lambda i: (i, 0)),
        out_shape=jax.ShapeDtypeStruct((L, D), BF16),
        compiler_params=_params("arbitrary"),
        name="rmsnorm",
    )(x, w3)


def _proj_kernel(a_ref, w_ref, cos_ref, sin_ref, qw_ref, kw_ref, o_ref, wbf_ref, *, tn, d_model):
    j = pl.program_id(0)

    @pl.when(pl.program_id(1) == 0)
    def _cast():
        wbf_ref[...] = w_ref[...].astype(BF16)

    def matmul():
        return jnp.dot(a_ref[...], wbf_ref[...], preferred_element_type=F32)

    t = d_model // tn
    half = t // 2
    groups = tn // HEAD_DIM

    def rope(y):
        return y * cos_ref[...] + pltpu.roll(y, HEAD_DIM // 2, 1) * sin_ref[...]

    @pl.when(j < t)
    def _ret_qk():
        acc = matmul()
        s = jnp.where(j >= half, HEAD_DIM ** -0.5, 1.0).astype(F32)
        for c in range(groups):
            sl = slice(c * HEAD_DIM, (c + 1) * HEAD_DIM)
            o_ref[:, sl] = (rope(acc[:, sl]) * s).astype(o_ref.dtype)

    @pl.when(((j >= t) & (j < 2 * t)) | ((j >= 5 * t) & (j < 6 * t)))
    def _plain():
        o_ref[...] = matmul().astype(o_ref.dtype)

    @pl.when((j >= 2 * t) & (j < 3 * t))
    def _silu():
        acc = matmul()
        o_ref[...] = (acc * jax.nn.sigmoid(acc)).astype(o_ref.dtype)

    @pl.when((j >= 3 * t) & (j < 5 * t))
    def _diff_qk():
        acc = matmul()
        is_q = (j < 4 * t).astype(F32)
        w = is_q * (qw_ref[...] * (HEAD_DIM ** -0.5 * LOG2E)) + (1.0 - is_q) * kw_ref[...]
        mean_w = jnp.full((2 * HEAD_DIM, HEAD_DIM), 1.0 / HEAD_DIM, BF16)
        for c in range(groups):
            sl = slice(c * HEAD_DIM, (c + 1) * HEAD_DIM)
            y = acc[:, sl]
            sq = y * y
            hi = sq.astype(BF16)
            lo = (sq - hi.astype(F32)).astype(BF16)
            ms = jnp.dot(jnp.concatenate([hi, lo], axis=1), mean_w, preferred_element_type=F32)
            y = y * lax.rsqrt(ms + NORM_EPS) * w
            o_ref[:, sl] = rope(y).astype(o_ref.dtype)

    @pl.when(j >= 6 * t)
    def _sigmoid():
        o_ref[...] = jax.nn.sigmoid(matmul()).astype(o_ref.dtype)


def _in_proj(xn, w_in, cos, sin_s, qw3, kw3, layer, tm, tn):
    L, D = xn.shape
    n_cols = w_in.shape[-1]
    tm = min(tm, L)
    tn = min(tn, D // 2)
    return pl.pallas_call(
        functools.partial(_proj_kernel, tn=tn, d_model=D),
        grid=(n_cols // tn, L // tm),
        in_specs=[pl.BlockSpec((tm, D), lambda j, i: (i, 0)),
                  pl.BlockSpec((None, D, tn), lambda j, i: (layer, 0, j)),
                  pl.BlockSpec((tm, HEAD_DIM), lambda j, i: (i, 0)),
                  pl.BlockSpec((tm, HEAD_DIM), lambda j, i: (i, 0)),
                  pl.BlockSpec((None, 1, HEAD_DIM), lambda j, i: (layer, 0, 0)),
                  pl.BlockSpec((None, 1, HEAD_DIM), lambda j, i: (layer, 0, 0))],
        out_specs=pl.BlockSpec((tm, tn), lambda j, i: (i, j)),
        out_shape=jax.ShapeDtypeStruct((L, n_cols), BF16),
        scratch_shapes=[pltpu.VMEM((D, tn), BF16)],
        compiler_params=_params("arbitrary", "arbitrary"),
        name="in_proj",
    )(xn, w_in, cos, sin_s, qw3, kw3)


def _log_sigmoid(x):
    return jnp.minimum(x, 0.0) - jnp.log1p(jnp.exp(-jnp.abs(x)))


def _retention_kernel(lf_ref, lb_ref, q_ref, k_ref, v_ref, g_ref, gnw_ref, o_ref,
                      state_ref, dmask_ref, wq_ref, wk_ref, *, chunk, n_chunks):
    C, N = chunk, n_chunks
    lgf = _log_sigmoid(lf_ref[...])
    lgb = _log_sigmoid(lb_ref[...])
    lgf1, lgb1 = lgf[:, :1], lgb[:, :1]

    row = lax.broadcasted_iota(jnp.int32, (C, HEAD_DIM), 0).astype(F32)
    wq_ref[0] = jnp.exp((row + 1.0) * lgf)
    wq_ref[1] = jnp.exp((C - row) * lgb)
    wk_ref[0] = jnp.exp((C - 1.0 - row) * lgf)
    wk_ref[1] = jnp.exp(row * lgb)
    r = lax.broadcasted_iota(jnp.int32, (C, C), 0)
    c = lax.broadcasted_iota(jnp.int32, (C, C), 1)
    diff = (r - c).astype(F32)
    dmask_ref[...] = jnp.where(diff > 0, jnp.exp(jnp.maximum(diff, 0.0) * lgf1),
                               jnp.where(diff < 0, jnp.exp(jnp.maximum(-diff, 0.0) * lgb1), 2.0))
    dec_f = jnp.exp(C * lgf1)
    dec_b = jnp.exp(C * lgb1)

    def rows(n):
        return pl.ds(pl.multiple_of(n * C, C), C)

    def states(t, carry):
        sf, sb = carry
        nb = N - 1 - t
        kf = (k_ref[rows(t), :].astype(F32) * wk_ref[0]).astype(BF16)
        kb = (k_ref[rows(nb), :].astype(F32) * wk_ref[1]).astype(BF16)
        kvf = lax.dot_general(kf, v_ref[rows(t), :], _TN, preferred_element_type=F32)
        kvb = lax.dot_general(kb, v_ref[rows(nb), :], _TN, preferred_element_type=F32)
        state_ref[t, 0:HEAD_DIM, :] = sf.astype(BF16)
        state_ref[nb, HEAD_DIM:2 * HEAD_DIM, :] = sb.astype(BF16)
        return sf * dec_f + kvf, sb * dec_b + kvb

    zero = jnp.zeros((HEAD_DIM, V_HEAD_DIM), F32)
    lax.fori_loop(0, N, states, (zero, zero), unroll=min(N, RET_UNROLL))

    def chunk_out(n, carry):
        q = q_ref[rows(n), :]
        s = lax.dot_general(q, k_ref[rows(n), :], _NT, preferred_element_type=F32)
        sd = (s * dmask_ref[...]).astype(BF16)
        qf = q.astype(F32)
        qcat = jnp.concatenate([qf * wq_ref[0], qf * wq_ref[1]], axis=1).astype(BF16)
        o = (jnp.dot(sd, v_ref[rows(n), :], preferred_element_type=F32)
             + jnp.dot(qcat, state_ref[n], preferred_element_type=F32))
        mu = jnp.mean(o, axis=-1, keepdims=True)
        xc = o - mu
        var = jnp.mean(xc * xc, axis=-1, keepdims=True)
        y = xc * lax.rsqrt(var + GN_EPS) * gnw_ref[...]
        o_ref[rows(n), :] = (g_ref[rows(n), :].astype(F32) * y).astype(o_ref.dtype)
        return carry

    lax.fori_loop(0, N, chunk_out, 0, unroll=min(N, RET_UNROLL))


def _retention(proj, lf4, lb4, gnw4, layer, d_model):
    L = proj.shape[0]
    H = d_model // V_HEAD_DIM
    C = min(RET_CHUNK, L)
    N = L // C
    qk_blocks = d_model // 2 // HEAD_DIM
    v_blocks = d_model // V_HEAD_DIM
    return pl.pallas_call(
        functools.partial(_retention_kernel, chunk=C, n_chunks=N),
        grid=(H,),
        in_specs=[pl.BlockSpec((None, None, 1, HEAD_DIM), lambda h: (layer, h, 0, 0)),
                  pl.BlockSpec((None, None, 1, HEAD_DIM), lambda h: (layer, h, 0, 0)),
                  pl.BlockSpec((L, HEAD_DIM), lambda h: (0, h)),
                  pl.BlockSpec((L, HEAD_DIM), lambda h: (0, qk_blocks + h)),
                  pl.BlockSpec((L, V_HEAD_DIM), lambda h: (0, v_blocks + h)),
                  pl.BlockSpec((L, V_HEAD_DIM), lambda h: (0, 2 * v_blocks + h)),
                  pl.BlockSpec((None, None, 1, V_HEAD_DIM), lambda h: (layer, h, 0, 0))],
        out_specs=pl.BlockSpec((L, V_HEAD_DIM), lambda h: (0, h)),
        out_shape=jax.ShapeDtypeStruct((L, d_model), BF16),
        scratch_shapes=[pltpu.VMEM((N, 2 * HEAD_DIM, V_HEAD_DIM), BF16),
                        pltpu.VMEM((C, C), F32),
                        pltpu.VMEM((2, C, HEAD_DIM), F32),
                        pltpu.VMEM((2, C, HEAD_DIM), F32)],
        compiler_params=_params("arbitrary"),
        name="retention",
    )(lf4, lb4, proj, proj, proj, proj, gnw4)


def _attn_kernel(lam_ref, q_ref, k_ref, v_ref, w_ref, o_ref, acc_ref, l_ref, shift_ref, knorm_ref,
                 *, tk, n_kv, lambda_init):
    tq = q_ref.shape[0]
    L = k_ref.shape[0]
    q = q_ref[...]
    qs = (q[:, :HEAD_DIM], q[:, HEAD_DIM:])

    @pl.when(pl.program_id(1) == 0)
    def _key_norms():
        rc = min(L, 512)

        def kbody(c, carry):
            kf = k_ref[pl.ds(pl.multiple_of(c * rc, rc), rc), :].astype(F32)
            kk = kf * kf
            return tuple(jnp.maximum(carry[t], jnp.sum(kk[:, t * HEAD_DIM:(t + 1) * HEAD_DIM],
                                                        axis=-1, keepdims=True)) for t in range(2))

        z = jnp.zeros((rc, 1), F32)
        k2 = lax.fori_loop(0, L // rc, kbody, (z, z))
        for t in range(2):
            kn = jnp.sqrt(jnp.max(k2[t], axis=0, keepdims=True))
            knorm_ref[t] = jnp.broadcast_to(kn, (1, HEAD_DIM))

    qf = q.astype(F32)
    bound = None
    for t in range(2):
        qt = qf[:, t * HEAD_DIM:(t + 1) * HEAD_DIM]
        qn = jnp.sqrt(jnp.sum(qt * qt, axis=-1, keepdims=True))
        shift = qn * knorm_ref[t][:, :1] * SHIFT_MARGIN
        shift_ref[t] = jnp.broadcast_to(shift, (tq, HEAD_DIM))
        bound = jnp.max(shift) if bound is None else jnp.maximum(bound, jnp.max(shift))

    def kv_rows(j):
        return pl.ds(pl.multiple_of(j * tk, tk), tk)

    def finish(l1, l2):
        lp = lam_ref[...]
        lam = (jnp.exp(jnp.sum(lp[0:1] * lp[1:2], axis=-1, keepdims=True))
               - jnp.exp(jnp.sum(lp[2:3] * lp[3:4], axis=-1, keepdims=True)) + lambda_init)
        o = acc_ref[0] * (1.0 / l1) - lam * (acc_ref[1] * (1.0 / l2))
        ms = jnp.mean(o * o, axis=-1, keepdims=True)
        y = o * lax.rsqrt(ms + NORM_EPS) * w_ref[...] * (1.0 - lambda_init)
        o_ref[...] = y.astype(o_ref.dtype)

    @pl.when(bound <= MAX_STATIC_SHIFT)
    def _static_shift():
        acc_ref[...] = jnp.zeros_like(acc_ref)
        l_ref[...] = jnp.zeros_like(l_ref)

        def body(j, carry):
            k = k_ref[kv_rows(j), :]
            v = v_ref[kv_rows(j), :]
            for t in range(2):
                s = lax.dot_general(qs[t], k[:, t * HEAD_DIM:(t + 1) * HEAD_DIM], _NT,
                                    preferred_element_type=F32)
                shift = shift_ref[t]
                lsum = l_ref[t]
                ps = []
                for c in range(tk // HEAD_DIM):
                    pc = jnp.exp2(s[:, c * HEAD_DIM:(c + 1) * HEAD_DIM] - shift)
                    lsum = lsum + pc
                    ps.append(pc.astype(BF16))
                l_ref[t] = lsum
                acc_ref[t] += jnp.dot(jnp.concatenate(ps, axis=1), v, preferred_element_type=F32)
            return carry

        lax.fori_loop(0, n_kv, body, 0)
        finish(jnp.sum(l_ref[0], axis=-1, keepdims=True), jnp.sum(l_ref[1], axis=-1, keepdims=True))

    @pl.when(bound > MAX_STATIC_SHIFT)
    def _online():
        acc_ref[...] = jnp.zeros_like(acc_ref)

        def body(j, carry):
            k = k_ref[kv_rows(j), :]
            v = v_ref[kv_rows(j), :]
            out = []
            for t in range(2):
                m, l = carry[2 * t], carry[2 * t + 1]
                s = lax.dot_general(qs[t], k[:, t * HEAD_DIM:(t + 1) * HEAD_DIM], _NT,
                                    preferred_element_type=F32)
                m_new = jnp.maximum(m, jnp.max(s, axis=-1, keepdims=True))
                alpha = jnp.exp2(m - m_new)
                p = jnp.exp2(s - m_new)
                l_new = alpha * l + jnp.sum(p, axis=-1, keepdims=True)
                acc_ref[t] = alpha * acc_ref[t] + jnp.dot(p.astype(BF16), v, preferred_element_type=F32)
                out += [m_new, l_new]
            return tuple(out)

        m0 = jnp.full((tq, 1), NEG_BIG, F32)
        l0 = jnp.zeros((tq, 1), F32)
        _, l1, _, l2 = lax.fori_loop(0, n_kv, body, (m0, l0, m0, l0))
        finish(l1, l2)


def _diff_attention(proj, diff_lambda, subln3, layer, d_model, tq, tk):
    L = proj.shape[0]
    H = d_model // V_HEAD_DIM
    tq = min(tq, L)
    tk = min(tk, L)
    vb = d_model // V_HEAD_DIM
    lambda_init = 0.8 - 0.6 * math.exp(-0.3 * layer)
    return pl.pallas_call(
        functools.partial(_attn_kernel, tk=tk, n_kv=L // tk, lambda_init=lambda_init),
        grid=(H, L // tq),
        in_specs=[pl.BlockSpec((None, 4, HEAD_DIM), lambda h, i: (layer, 0, 0)),
                  pl.BlockSpec((tq, V_HEAD_DIM), lambda h, i: (i, 3 * vb + h)),
                  pl.BlockSpec((L, V_HEAD_DIM), lambda h, i: (0, 4 * vb + h)),
                  pl.BlockSpec((L, V_HEAD_DIM), lambda h, i: (0, 5 * vb + h)),
                  pl.BlockSpec((None, 1, V_HEAD_DIM), lambda h, i: (layer, 0, 0))],
        out_specs=pl.BlockSpec((tq, V_HEAD_DIM), lambda h, i: (i, h)),
        out_shape=jax.ShapeDtypeStruct((L, d_model), BF16),
        scratch_shapes=[pltpu.VMEM((2, tq, V_HEAD_DIM), F32),
                        pltpu.VMEM((2, tq, HEAD_DIM), F32),
                        pltpu.VMEM((2, tq, HEAD_DIM), F32),
                        pltpu.VMEM((2, 1, HEAD_DIM), F32)],
        compiler_params=_params("arbitrary", "arbitrary"),
        name="diff_attention",
    )(diff_lambda, proj, proj, proj, subln3)


def _merge_kernel(a1_ref, a2_ref, w1_ref, w2_ref, g1_ref, g2_ref, o_ref, w1bf_ref, w2bf_ref):
    @pl.when(pl.program_id(1) == 0)
    def _cast():
        w1bf_ref[...] = w1_ref[...].astype(BF16)
        w2bf_ref[...] = w2_ref[...].astype(BF16)

    y1 = jnp.dot(a1_ref[...], w1bf_ref[...], preferred_element_type=F32)
    y2 = jnp.dot(a2_ref[...], w2bf_ref[...], preferred_element_type=F32)
    o_ref[...] = (g1_ref[...].astype(F32) * y1 + g2_ref[...].astype(F32) * y2).astype(o_ref.dtype)


def _merge(ret, att, w_ret_out, w_diff_out, proj, layer, tm, tn):
    L, D = ret.shape
    tm = min(tm, L)
    tn = min(tn, D)
    gb = D // tn
    return pl.pallas_call(
        _merge_kernel,
        grid=(D // tn, L // tm),
        in_specs=[pl.BlockSpec((tm, D), lambda j, i: (i, 0)),
                  pl.BlockSpec((tm, D), lambda j, i: (i, 0)),
                  pl.BlockSpec((None, D, tn), lambda j, i: (layer, 0, j)),
                  pl.BlockSpec((None, D, tn), lambda j, i: (layer, 0, j)),
                  pl.BlockSpec((tm, tn), lambda j, i: (i, 6 * gb + j)),
                  pl.BlockSpec((tm, tn), lambda j, i: (i, 7 * gb + j))],
        out_specs=pl.BlockSpec((tm, tn), lambda j, i: (i, j)),
        out_shape=jax.ShapeDtypeStruct((L, D), BF16),
        scratch_shapes=[pltpu.VMEM((D, tn), BF16), pltpu.VMEM((D, tn), BF16)],
        compiler_params=_params("arbitrary", "arbitrary"),
        name="merge",
    )(ret, att, w_ret_out, w_diff_out, proj, proj)


def _matmul_kernel(a_ref, w_ref, *rest, residual, relu2):
    if residual:
        x_ref, o_ref, wbf_ref = rest
    else:
        o_ref, wbf_ref = rest

    @pl.when(pl.program_id(1) == 0)
    def _cast():
        wbf_ref[...] = w_ref[...].astype(BF16)

    y = jnp.dot(a_ref[...], wbf_ref[...], preferred_element_type=F32)
    if relu2:
        y = jnp.square(jnp.maximum(y, 0.0))
    if residual:
        y = x_ref[...] + y
    o_ref[...] = y.astype(o_ref.dtype)


def _matmul(a, w, layer, tm, tn, *, x=None, relu2=False, out_dtype=F32, name):
    L, K = a.shape
    n_cols = w.shape[-1]
    tm = min(tm, L)
    tn = min(tn, n_cols)
    in_specs = [pl.BlockSpec((tm, K), lambda j, i: (i, 0)),
                pl.BlockSpec((None, K, tn), lambda j, i: (layer, 0, j))]
    args = [a, w]
    if x is not None:
        in_specs.append(pl.BlockSpec((tm, tn), lambda j, i: (i, j)))
        args.append(x)
    return pl.pallas_call(
        functools.partial(_matmul_kernel, residual=x is not None, relu2=relu2),
        grid=(n_cols // tn, L // tm),
        in_specs=in_specs,
        out_specs=pl.BlockSpec((tm, tn), lambda j, i: (i, j)),
        out_shape=jax.ShapeDtypeStruct((L, n_cols), out_dtype),
        scratch_shapes=[pltpu.VMEM((K, tn), BF16)],
        compiler_params=_params("arbitrary", "arbitrary"),
        name=name,
    )(*args)


def _matmul_kgrid_kernel(a_ref, w_ref, x_ref, o_ref):
    @pl.when(pl.program_id(2) == 0)
    def _init():
        o_ref[...] = x_ref[...]

    o_ref[...] += jnp.dot(a_ref[...], w_ref[...].astype(BF16), preferred_element_type=F32)


def _matmul_kgrid(a, w, layer, x, tm, tn, tk, *, name):
    L, K = a.shape
    n_cols = w.shape[-1]
    tm, tn, tk = min(tm, L), min(tn, n_cols), min(tk, K)
    return pl.pallas_call(
        _matmul_kgrid_kernel,
        grid=(n_cols // tn, L // tm, K // tk),
        in_specs=[pl.BlockSpec((tm, tk), lambda j, i, k: (i, k)),
                  pl.BlockSpec((None, tk, tn), lambda j, i, k: (layer, k, j)),
                  pl.BlockSpec((tm, tn), lambda j, i, k: (i, j))],
        out_specs=pl.BlockSpec((tm, tn), lambda j, i, k: (i, j)),
        out_shape=jax.ShapeDtypeStruct((L, n_cols), F32),
        compiler_params=_params("arbitrary", "arbitrary", "arbitrary"),
        name=name,
    )(a, w, x)


def _rope_tables(L):
    inv = 1.0 / (ROPE_THETA ** (jnp.arange(0, HEAD_DIM, 2, dtype=F32) / HEAD_DIM))
    ang = jnp.arange(L, dtype=F32)[:, None] * inv[None, :]
    ang = jnp.concatenate([ang, ang], axis=-1)
    sign = jnp.where(jnp.arange(HEAD_DIM) < HEAD_DIM // 2, -1.0, 1.0).astype(F32)
    return jnp.cos(ang), jnp.sin(ang) * sign[None, :]


def kernel(x, norm_mix_w, w_in, ret_decay_fwd, ret_decay_bwd, ret_gn_w, w_ret_out, q_norm_w, k_norm_w,
           diff_lambda, diff_subln_w, w_diff_out, w_out, norm_mlp_w, w_mlp_in, w_mlp_out):
    B, L, D = x.shape
    assert B == 1 and D % V_HEAD_DIM == 0
    depth = w_in.shape[0]
    H = D // V_HEAD_DIM
    cos, sin_s = _rope_tables(L)

    norm_mix3 = norm_mix_w.reshape(depth, 1, D)
    norm_mlp3 = norm_mlp_w.reshape(depth, 1, D)
    qw3 = q_norm_w.reshape(depth, 1, HEAD_DIM)
    kw3 = k_norm_w.reshape(depth, 1, HEAD_DIM)
    subln3 = diff_subln_w.reshape(depth, 1, V_HEAD_DIM)
    gnw4 = ret_gn_w.reshape(depth, H, 1, V_HEAD_DIM)
    lf4 = jnp.broadcast_to(ret_decay_fwd[:, :, None, None], (depth, H, 1, HEAD_DIM))
    lb4 = jnp.broadcast_to(ret_decay_bwd[:, :, None, None], (depth, H, 1, HEAD_DIM))

    h = x.reshape(L, D)
    for layer in range(depth):
        xn = _rmsnorm(h, norm_mix3, layer, 512)
        proj = _in_proj(xn, w_in, cos, sin_s, qw3, kw3, layer, 1024, 1024)
        ret = _retention(proj, lf4, lb4, gnw4, layer, D)
        att = _diff_attention(proj, diff_lambda, subln3, layer, D, 1024, 1024)
        merged = _merge(ret, att, w_ret_out, w_diff_out, proj, layer, 512, 512)
        h = _matmul(merged, w_out, layer, 1024, 512, x=h, name="out_proj")
        hn = _rmsnorm(h, norm_mlp3, layer, 512)
        u = _matmul(hn, w_mlp_in, layer, 1024, 1024, relu2=True, out_dtype=BF16, name="mlp_in")
        h = _matmul_kgrid(u, w_mlp_out, layer, h, 2048, 1024, 512, name="mlp_out")
    return h.reshape(B, L, D)
```

```python
import functools
import math

import jax
import jax.numpy as jnp
from jax import lax
from jax.experimental import pallas as pl
from jax.experimental.pallas import tpu as pltpu

HEAD_DIM = 128
V_HEAD_DIM = 2 * HEAD_DIM
NORM_EPS = 1e-6
GN_EPS = 1e-5
ROPE_THETA = 10000.0
LOG2E = 1.4426950408889634
RET_CHUNK = 256
RET_UNROLL = 4
NEG_BIG = -1e30
MAX_STATIC_SHIFT = 50.0
SHIFT_MARGIN = 1.01
VMEM_LIMIT_V7X = 56 * 1024 * 1024

F32 = jnp.float32
BF16 = jnp.bfloat16
_NT = (((1,), (1,)), ((), ()))
_TN = (((0,), (0,)), ((), ()))


def _params(*sem):
    return pltpu.CompilerParams(dimension_semantics=sem, vmem_limit_bytes=VMEM_LIMIT_V7X)


def _rmsnorm_kernel(x_ref, w_ref, o_ref):
    x = x_ref[...]
    ms = jnp.mean(x * x, axis=-1, keepdims=True)
    o_ref[...] = (x * lax.rsqrt(ms + NORM_EPS) * w_ref[...]).astype(o_ref.dtype)


def _rmsnorm(x, w3, layer, tm):
    L, D = x.shape
    tm = min(tm, L)
    return pl.pallas_call(
        _rmsnorm_kernel,
        grid=(L // tm,),
        in_specs=[pl.BlockSpec((tm, D), lambda i: (i, 0)),
                  pl.BlockSpec((None, 1, D), lambda i: (layer, 0, 0))],
        out_specs=pl.BlockSpec((tm, D), lambda i: (i, 0)),
        out_shape=jax.ShapeDtypeStruct((L, D), BF16),
        compiler_params=_params("arbitrary"),
        name="rmsnorm",
    )(x, w3)


def _proj_kernel(a_ref, w_ref, cos_ref, sin_ref, qw_ref, kw_ref, o_ref, wbf_ref, *, tn, d_model):
    j = pl.program_id(0)

    @pl.when(pl.program_id(1) == 0)
    def _cast():
        wbf_ref[...] = w_ref[...].astype(BF16)

    def matmul():
        return jnp.dot(a_ref[...], wbf_ref[...], preferred_element_type=F32)

    t = d_model // tn
    half = t // 2
    groups = tn // HEAD_DIM

    def rope(y):
        return y * cos_ref[...] + pltpu.roll(y, HEAD_DIM // 2, 1) * sin_ref[...]

    @pl.when(j < t)
    def _ret_qk():
        acc = matmul()
        s = jnp.where(j >= half, HEAD_DIM ** -0.5, 1.0).astype(F32)
        for c in range(groups):
            sl = slice(c * HEAD_DIM, (c + 1) * HEAD_DIM)
            o_ref[:, sl] = (rope(acc[:, sl]) * s).astype(o_ref.dtype)

    @pl.when(((j >= t) & (j < 2 * t)) | ((j >= 5 * t) & (j < 6 * t)))
    def _plain():
        o_ref[...] = matmul().astype(o_ref.dtype)

    @pl.when((j >= 2 * t) & (j < 3 * t))
    def _silu():
        acc = matmul()
        o_ref[...] = (acc * jax.nn.sigmoid(acc)).astype(o_ref.dtype)

    @pl.when((j >= 3 * t) & (j < 5 * t))
    def _diff_qk():
        acc = matmul()
        is_q = (j < 4 * t).astype(F32)
        w = is_q * (qw_ref[...] * (HEAD_DIM ** -0.5 * LOG2E)) + (1.0 - is_q) * kw_ref[...]
        mean_w = jnp.full((2 * HEAD_DIM, HEAD_DIM), 1.0 / HEAD_DIM, BF16)
        for c in range(groups):
            sl = slice(c * HEAD_DIM, (c + 1) * HEAD_DIM)
            y = acc[:, sl]
            sq = y * y
            hi = sq.astype(BF16)
            lo = (sq - hi.astype(F32)).astype(BF16)
            ms = jnp.dot(jnp.concatenate([hi, lo], axis=1), mean_w, preferred_element_type=F32)
            y = y * lax.rsqrt(ms + NORM_EPS) * w
            o_ref[:, sl] = rope(y).astype(o_ref.dtype)

    @pl.when(j >= 6 * t)
    def _sigmoid():
        o_ref[...] = jax.nn.sigmoid(matmul()).astype(o_ref.dtype)


def _in_proj(xn, w_in, cos, sin_s, qw3, kw3, layer, tm, tn):
    L, D = xn.shape
    n_cols = w_in.shape[-1]
    tm = min(tm, L)
    tn = min(tn, D // 2)
    return pl.pallas_call(
        functools.partial(_proj_kernel, tn=tn, d_model=D),
        grid=(n_cols // tn, L // tm),
        in_specs=[pl.BlockSpec((tm, D), lambda j, i: (i, 0)),
                  pl.BlockSpec((None, D, tn), lambda j, i: (layer, 0, j)),
                  pl.BlockSpec((tm, HEAD_DIM), lambda j, i: (i, 0)),
                  pl.BlockSpec((tm, HEAD_DIM), lambda j, i: (i, 0)),
                  pl.BlockSpec((None, 1, HEAD_DIM), lambda j, i: (layer, 0, 0)),
                  pl.BlockSpec((None, 1, HEAD_DIM), lambda j, i: (layer, 0, 0))],
        out_specs=pl.BlockSpec((tm, tn), lambda j, i: (i, j)),
        out_shape=jax.ShapeDtypeStruct((L, n_cols), BF16),
        scratch_shapes=[pltpu.VMEM((D, tn), BF16)],
        compiler_params=_params("arbitrary", "arbitrary"),
        name="in_proj",
    )(xn, w_in, cos, sin_s, qw3, kw3)


def _log_sigmoid(x):
    return jnp.minimum(x, 0.0) - jnp.log1p(jnp.exp(-jnp.abs(x)))


def _retention_kernel(lf_ref, lb_ref, q_ref, k_ref, v_ref, g_ref, gnw_ref, o_ref,
                      state_ref, dmask_ref, wq_ref, wk_ref, *, chunk, n_chunks):
    C, N = chunk, n_chunks
    lgf = _log_sigmoid(lf_ref[...])
    lgb = _log_sigmoid(lb_ref[...])
    lgf1, lgb1 = lgf[:, :1], lgb[:, :1]

    row = lax.broadcasted_iota(jnp.int32, (C, HEAD_DIM), 0).astype(F32)
    wq_ref[0] = jnp.exp((row + 1.0) * lgf)
    wq_ref[1] = jnp.exp((C - row) * lgb)
    wk_ref[0] = jnp.exp((C - 1.0 - row) * lgf)
    wk_ref[1] = jnp.exp(row * lgb)
    r = lax.broadcasted_iota(jnp.int32, (C, C), 0)
    c = lax.broadcasted_iota(jnp.int32, (C, C), 1)
    diff = (r - c).astype(F32)
    dmask_ref[...] = jnp.where(diff > 0, jnp.exp(jnp.maximum(diff, 0.0) * lgf1),
                               jnp.where(diff < 0, jnp.exp(jnp.maximum(-diff, 0.0) * lgb1), 2.0))
    dec_f = jnp.exp(C * lgf1)
    dec_b = jnp.exp(C * lgb1)

    def rows(n):
        return pl.ds(pl.multiple_of(n * C, C), C)

    def states(t, carry):
        sf, sb = carry
        nb = N - 1 - t
        kf = (k_ref[rows(t), :].astype(F32) * wk_ref[0]).astype(BF16)
        kb = (k_ref[rows(nb), :].astype(F32) * wk_ref[1]).astype(BF16)
        kvf = lax.dot_general(kf, v_ref[rows(t), :], _TN, preferred_element_type=F32)
        kvb = lax.dot_general(kb, v_ref[rows(nb), :], _TN, preferred_element_type=F32)
        state_ref[t, 0:HEAD_DIM, :] = sf.astype(BF16)
        state_ref[nb, HEAD_DIM:2 * HEAD_DIM, :] = sb.astype(BF16)
        return sf * dec_f + kvf, sb * dec_b + kvb

    zero = jnp.zeros((HEAD_DIM, V_HEAD_DIM), F32)
    lax.fori_loop(0, N, states, (zero, zero), unroll=min(N, RET_UNROLL))

    def chunk_out(n, carry):
        q = q_ref[rows(n), :]
        s = lax.dot_general(q, k_ref[rows(n), :], _NT, preferred_element_type=F32)
        sd = (s * dmask_ref[...]).astype(BF16)
        qf = q.astype(F32)
        qcat = jnp.concatenate([qf * wq_ref[0], qf * wq_ref[1]], axis=1).astype(BF16)
        o = (jnp.dot(sd, v_ref[rows(n), :], preferred_element_type=F32)
             + jnp.dot(qcat, state_ref[n], preferred_element_type=F32))
        mu = jnp.mean(o, axis=-1, keepdims=True)
        xc = o - mu
        var = jnp.mean(xc * xc, axis=-1, keepdims=True)
        y = xc * lax.rsqrt(var + GN_EPS) * gnw_ref[...]
        o_ref[rows(n), :] = (g_ref[rows(n), :].astype(F32) * y).astype(o_ref.dtype)
        return carry

    lax.fori_loop(0, N, chunk_out, 0, unroll=min(N, RET_UNROLL))


def _retention(proj, lf4, lb4, gnw4, layer, d_model):
    L = proj.shape[0]
    H = d_model // V_HEAD_DIM
    C = min(RET_CHUNK, L)
    N = L // C
    qk_blocks = d_model // 2 // HEAD_DIM
    v_blocks = d_model // V_HEAD_DIM
    return pl.pallas_call(
        functools.partial(_retention_kernel, chunk=C, n_chunks=N),
        grid=(H,),
        in_specs=[pl.BlockSpec((None, None, 1, HEAD_DIM), lambda h: (layer, h, 0, 0)),
                  pl.BlockSpec((None, None, 1, HEAD_DIM), lambda h: (layer, h, 0, 0)),
                  pl.BlockSpec((L, HEAD_DIM), lambda h: (0, h)),
                  pl.BlockSpec((L, HEAD_DIM), lambda h: (0, qk_blocks + h)),
                  pl.BlockSpec((L, V_HEAD_DIM), lambda h: (0, v_blocks + h)),
                  pl.BlockSpec((L, V_HEAD_DIM), lambda h: (0, 2 * v_blocks + h)),
                  pl.BlockSpec((None, None, 1, V_HEAD_DIM), lambda h: (layer, h, 0, 0))],
        out_specs=pl.BlockSpec((L, V_HEAD_DIM), lambda h: (0, h)),
        out_shape=jax.ShapeDtypeStruct((L, d_model), BF16),
        scratch_shapes=[pltpu.VMEM((N, 2 * HEAD_DIM, V_HEAD_DIM), BF16),
                        pltpu.VMEM((C, C), F32),
                        pltpu.VMEM((2, C, HEAD_DIM), F32),
                        pltpu.VMEM((2, C, HEAD_DIM), F32)],
        compiler_params=_params("arbitrary"),
        name="retention",
    )(lf4, lb4, proj, proj, proj, proj, gnw4)


def _attn_kernel(lam_ref, q_ref, k_ref, v_ref, w_ref, o_ref, acc_ref, l_ref, shift_ref, knorm_ref,
                 *, tk, n_kv, lambda_init):
    tq = q_ref.shape[0]
    L = k_ref.shape[0]
    q = q_ref[...]
    qs = (q[:, :HEAD_DIM], q[:, HEAD_DIM:])

    @pl.when(pl.program_id(1) == 0)
    def _key_norms():
        rc = min(L, 512)

        def kbody(c, carry):
            kf = k_ref[pl.ds(pl.multiple_of(c * rc, rc), rc), :].astype(F32)
            kk = kf * kf
            return tuple(jnp.maximum(carry[t], jnp.sum(kk[:, t * HEAD_DIM:(t + 1) * HEAD_DIM],
                                                        axis=-1, keepdims=True)) for t in range(2))

        z = jnp.zeros((rc, 1), F32)
        k2 = lax.fori_loop(0, L // rc, kbody, (z, z))
        for t in range(2):
            kn = jnp.sqrt(jnp.max(k2[t], axis=0, keepdims=True))
            knorm_ref[t] = jnp.broadcast_to(kn, (1, HEAD_DIM))

    qf = q.astype(F32)
    bound = None
    for t in range(2):
        qt = qf[:, t * HEAD_DIM:(t + 1) * HEAD_DIM]
        qn = jnp.sqrt(jnp.sum(qt * qt, axis=-1, keepdims=True))
        shift = qn * knorm_ref[t][:, :1] * SHIFT_MARGIN
        shift_ref[t] = jnp.broadcast_to(shift, (tq, HEAD_DIM))
        bound = jnp.max(shift) if bound is None else jnp.maximum(bound, jnp.max(shift))

    def kv_rows(j):
        return pl.ds(pl.multiple_of(j * tk, tk), tk)

    def finish(l1, l2):
        lp = lam_ref[...]
        lam = (jnp.exp(jnp.sum(lp[0:1] * lp[1:2], axis=-1, keepdims=True))
               - jnp.exp(jnp.sum(lp[2:3] * lp[3:4], axis=-1, keepdims=True)) + lambda_init)
        o = acc_ref[0] * (1.0 / l1) - lam * (acc_ref[1] * (1.0 / l2))
        ms = jnp.mean(o * o, axis=-1, keepdims=True)
        y = o * lax.rsqrt(ms + NORM_EPS) * w_ref[...] * (1.0 - lambda_init)
        o_ref[...] = y.astype(o_ref.dtype)

    @pl.when(bound <= MAX_STATIC_SHIFT)
    def _static_shift():
        acc_ref[...] = jnp.zeros_like(acc_ref)
        l_ref[...] = jnp.zeros_like(l_ref)

        def body(j, carry):
            k = k_ref[kv_rows(j), :]
            v = v_ref[kv_rows(j), :]
            for t in range(2):
                s = lax.dot_general(qs[t], k[:, t * HEAD_DIM:(t + 1) * HEAD_DIM], _NT,
                                    preferred_element_type=F32)
                shift = shift_ref[t]
                lsum = l_ref[t]
                ps = []
                for c in range(tk // HEAD_DIM):
                    pc = jnp.exp2(s[:, c * HEAD_DIM:(c + 1) * HEAD_DIM] - shift)
                    lsum = lsum + pc
                    ps.append(pc.astype(BF16))
                l_ref[t] = lsum
                acc_ref[t] += jnp.dot(jnp.concatenate(ps, axis=1), v, preferred_element_type=F32)
            return carry

        lax.fori_loop(0, n_kv, body, 0, unroll=True)
        finish(jnp.sum(l_ref[0], axis=-1, keepdims=True), jnp.sum(l_ref[1], axis=-1, keepdims=True))

    @pl.when(bound > MAX_STATIC_SHIFT)
    def _online():
        acc_ref[...] = jnp.zeros_like(acc_ref)

        def body(j, carry):
            k = k_ref[kv_rows(j), :]
            v = v_ref[kv_rows(j), :]
            out = []
            for t in range(2):
                m, l = carry[2 * t], carry[2 * t + 1]
                s = lax.dot_general(qs[t], k[:, t * HEAD_DIM:(t + 1) * HEAD_DIM], _NT,
                                    preferred_element_type=F32)
                m_new = jnp.maximum(m, jnp.max(s, axis=-1, keepdims=True))
                alpha = jnp.exp2(m - m_new)
                p = jnp.exp2(s - m_new)
                l_new = alpha * l + jnp.sum(p, axis=-1, keepdims=True)
                acc_ref[t] = alpha * acc_ref[t] + jnp.dot(p.astype(BF16), v, preferred_element_type=F32)
                out += [m_new, l_new]
            return tuple(out)

        m0 = jnp.full((tq, 1), NEG_BIG, F32)
        l0 = jnp.zeros((tq, 1), F32)
        _, l1, _, l2 = lax.fori_loop(0, n_kv, body, (m0, l0, m0, l0))
        finish(l1, l2)


def _diff_attention(proj, diff_lambda, subln3, layer, d_model, tq, tk):
    L = proj.shape[0]
    H = d_model // V_HEAD_DIM
    tq = min(tq, L)
    tk = min(tk, L)
    vb = d_model // V_HEAD_DIM
    lambda_init = 0.8 - 0.6 * math.exp(-0.3 * layer)
    return pl.pallas_call(
        functools.partial(_attn_kernel, tk=tk, n_kv=L // tk, lambda_init=lambda_init),
        grid=(H, L // tq),
        in_specs=[pl.BlockSpec((None, 4, HEAD_DIM), lambda h, i: (layer, 0, 0)),
                  pl.BlockSpec((tq, V_HEAD_DIM), lambda h, i: (i, 3 * vb + h)),
                  pl.BlockSpec((L, V_HEAD_DIM), lambda h, i: (0, 4 * vb + h)),
                  pl.BlockSpec((L, V_HEAD_DIM), lambda h, i: (0, 5 * vb + h)),
                  pl.BlockSpec((None, 1, V_HEAD_DIM), lambda h, i: (layer, 0, 0))],
        out_specs=pl.BlockSpec((tq, V_HEAD_DIM), lambda h, i: (i, h)),
        out_shape=jax.ShapeDtypeStruct((L, d_model), BF16),
        scratch_shapes=[pltpu.VMEM((2, tq, V_HEAD_DIM), F32),
                        pltpu.VMEM((2, tq, HEAD_DIM), F32),
                        pltpu.VMEM((2, tq, HEAD_DIM), F32),
                        pltpu.VMEM((2, 1, HEAD_DIM), F32)],
        compiler_params=_params("arbitrary", "arbitrary"),
        name="diff_attention",
    )(diff_lambda, proj, proj, proj, subln3)


def _merge_kernel(a1_ref, a2_ref, w1_ref, w2_ref, g1_ref, g2_ref, o_ref, w1bf_ref, w2bf_ref):
    @pl.when(pl.program_id(1) == 0)
    def _cast():
        w1bf_ref[...] = w1_ref[...].astype(BF16)
        w2bf_ref[...] = w2_ref[...].astype(BF16)

    y1 = jnp.dot(a1_ref[...], w1bf_ref[...], preferred_element_type=F32)
    y2 = jnp.dot(a2_ref[...], w2bf_ref[...], preferred_element_type=F32)
    o_ref[...] = (g1_ref[...].astype(F32) * y1 + g2_ref[...].astype(F32) * y2).astype(o_ref.dtype)


def _merge(ret, att, w_ret_out, w_diff_out, proj, layer, tm, tn):
    L, D = ret.shape
    tm = min(tm, L)
    tn = min(tn, D)
    gb = D // tn
    return pl.pallas_call(
        _merge_kernel,
        grid=(D // tn, L // tm),
        in_specs=[pl.BlockSpec((tm, D), lambda j, i: (i, 0)),
                  pl.BlockSpec((tm, D), lambda j, i: (i, 0)),
                  pl.BlockSpec((None, D, tn), lambda j, i: (layer, 0, j)),
                  pl.BlockSpec((None, D, tn), lambda j, i: (layer, 0, j)),
                  pl.BlockSpec((tm, tn), lambda j, i: (i, 6 * gb + j)),
                  pl.BlockSpec((tm, tn), lambda j, i: (i, 7 * gb + j))],
        out_specs=pl.BlockSpec((tm, tn), lambda j, i: (i, j)),
        out_shape=jax.ShapeDtypeStruct((L, D), BF16),
        scratch_shapes=[pltpu.VMEM((D, tn), BF16), pltpu.VMEM((D, tn), BF16)],
        compiler_params=_params("arbitrary", "arbitrary"),
        name="merge",
    )(ret, att, w_ret_out, w_diff_out, proj, proj)


def _matmul_kernel(a_ref, w_ref, *rest, residual, relu2):
    if residual:
        x_ref, o_ref, wbf_ref = rest
    else:
        o_ref, wbf_ref = rest

    @pl.when(pl.program_id(1) == 0)
    def _cast():
        wbf_ref[...] = w_ref[...].astype(BF16)

    y = jnp.dot(a_ref[...], wbf_ref[...], preferred_element_type=F32)
    if relu2:
        y = jnp.square(jnp.maximum(y, 0.0))
    if residual:
        y = x_ref[...] + y
    o_ref[...] = y.astype(o_ref.dtype)


def _matmul(a, w, layer, tm, tn, *, x=None, relu2=False, out_dtype=F32, name):
    L, K = a.shape
    n_cols = w.shape[-1]
    tm = min(tm, L)
    tn = min(tn, n_cols)
    in_specs = [pl.BlockSpec((tm, K), lambda j, i: (i, 0)),
                pl.BlockSpec((None, K, tn), lambda j, i: (layer, 0, j))]
    args = [a, w]
    if x is not None:
        in_specs.append(pl.BlockSpec((tm, tn), lambda j, i: (i, j)))
        args.append(x)
    return pl.pallas_call(
        functools.partial(_matmul_kernel, residual=x is not None, relu2=relu2),
        grid=(n_cols // tn, L // tm),
        in_specs=in_specs,
        out_specs=pl.BlockSpec((tm, tn), lambda j, i: (i, j)),
        out_shape=jax.ShapeDtypeStruct((L, n_cols), out_dtype),
        scratch_shapes=[pltpu.VMEM((K, tn), BF16)],
        compiler_params=_params("arbitrary", "arbitrary"),
        name=name,
    )(*args)


def _matmul_kgrid_kernel(a_ref, w_ref, x_ref, o_ref):
    @pl.when(pl.program_id(2) == 0)
    def _init():
        o_ref[...] = x_ref[...]

    o_ref[...] += jnp.dot(a_ref[...], w_ref[...].astype(BF16), preferred_element_type=F32)


def _matmul_kgrid(a, w, layer, x, tm, tn, tk, *, name):
    L, K = a.shape
    n_cols = w.shape[-1]
    tm, tn, tk = min(tm, L), min(tn, n_cols), min(tk, K)
    return pl.pallas_call(
        _matmul_kgrid_kernel,
        grid=(n_cols // tn, L // tm, K // tk),
        in_specs=[pl.BlockSpec((tm, tk), lambda j, i, k: (i, k)),
                  pl.BlockSpec((None, tk, tn), lambda j, i, k: (layer, k, j)),
                  pl.BlockSpec((tm, tn), lambda j, i, k: (i, j))],
        out_specs=pl.BlockSpec((tm, tn), lambda j, i, k: (i, j)),
        out_shape=jax.ShapeDtypeStruct((L, n_cols), F32),
        compiler_params=_params("arbitrary", "arbitrary", "arbitrary"),
        name=name,
    )(a, w, x)


def _rope_tables(L):
    inv = 1.0 / (ROPE_THETA ** (jnp.arange(0, HEAD_DIM, 2, dtype=F32) / HEAD_DIM))
    ang = jnp.arange(L, dtype=F32)[:, None] * inv[None, :]
    ang = jnp.concatenate([ang, ang], axis=-1)
    sign = jnp.where(jnp.arange(HEAD_DIM) < HEAD_DIM // 2, -1.0, 1.0).astype(F32)
    return jnp.cos(ang), jnp.sin(ang) * sign[None, :]


def kernel(x, norm_mix_w, w_in, ret_decay_fwd, ret_decay_bwd, ret_gn_w, w_ret_out, q_norm_w, k_norm_w,
           diff_lambda, diff_subln_w, w_diff_out, w_out, norm_mlp_w, w_mlp_in, w_mlp_out):
    B, L, D = x.shape
    assert B == 1 and D % V_HEAD_DIM == 0
    depth = w_in.shape[0]
    H = D // V_HEAD_DIM
    cos, sin_s = _rope_tables(L)

    norm_mix3 = norm_mix_w.reshape(depth, 1, D)
    norm_mlp3 = norm_mlp_w.reshape(depth, 1, D)
    qw3 = q_norm_w.reshape(depth, 1, HEAD_DIM)
    kw3 = k_norm_w.reshape(depth, 1, HEAD_DIM)
    subln3 = diff_subln_w.reshape(depth, 1, V_HEAD_DIM)
    gnw4 = ret_gn_w.reshape(depth, H, 1, V_HEAD_DIM)
    lf4 = jnp.broadcast_to(ret_decay_fwd[:, :, None, None], (depth, H, 1, HEAD_DIM))
    lb4 = jnp.broadcast_to(ret_decay_bwd[:, :, None, None], (depth, H, 1, HEAD_DIM))

    h = x.reshape(L, D)
    for layer in range(depth):
        xn = _rmsnorm(h, norm_mix3, layer, 512)
        proj = _in_proj(xn, w_in, cos, sin_s, qw3, kw3, layer, 1024, 1024)
        ret = _retention(proj, lf4, lb4, gnw4, layer, D)
        att = _diff_attention(proj, diff_lambda, subln3, layer, D, 1024, 1024)
        merged = _merge(ret, att, w_ret_out, w_diff_out, proj, layer, 1024, 512)
        h = _matmul(merged, w_out, layer, 1024, 1024, x=h, name="out_proj")
        hn = _rmsnorm(h, norm_mlp3, layer, 512)
        u = _matmul(hn, w_mlp_in, layer, 1024, 1024, relu2=True, out_dtype=BF16, name="mlp_in")
        h = _matmul_kgrid(u, w_mlp_out, layer, h, 2048, 1024, 512, name="mlp_out")
    return h.reshape(B, L, D)
```

```python
import functools
import math

import jax
import jax.numpy as jnp
from jax import lax
from jax.experimental import pallas as pl
from jax.experimental.pallas import tpu as pltpu

HEAD_DIM = 128
V_HEAD_DIM = 2 * HEAD_DIM
NORM_EPS = 1e-6
GN_EPS = 1e-5
ROPE_THETA = 10000.0
LOG2E = 1.4426950408889634
RET_CHUNK = 256
RET_UNROLL = 4
NEG_BIG = -1e30
MAX_EXP2_ARG = 50.0
SCORE_MARGIN = 1.01
VMEM_LIMIT_V7X = 56 * 1024 * 1024

F32 = jnp.float32
BF16 = jnp.bfloat16
_NT = (((1,), (1,)), ((), ()))
_TN = (((0,), (0,)), ((), ()))


def _params(*sem):
    return pltpu.CompilerParams(dimension_semantics=sem, vmem_limit_bytes=VMEM_LIMIT_V7X)


def _rmsnorm_kernel(x_ref, w_ref, o_ref):
    x = x_ref[...]
    ms = jnp.mean(x * x, axis=-1, keepdims=True)
    o_ref[...] = (x * lax.rsqrt(ms + NORM_EPS) * w_ref[...]).astype(o_ref.dtype)


def _rmsnorm(x, w3, layer, tm):
    L, D = x.shape
    tm = min(tm, L)
    return pl.pallas_call(
        _rmsnorm_kernel,
        grid=(L // tm,),
        in_specs=[pl.BlockSpec((tm, D), lambda i: (i, 0)),
                  pl.BlockSpec((None, 1, D), lambda i: (layer, 0, 0))],
        out_specs=pl.BlockSpec((tm, D), lambda i: (i, 0)),
        out_shape=jax.ShapeDtypeStruct((L, D), BF16),
        compiler_params=_params("arbitrary"),
        name="rmsnorm",
    )(x, w3)


def _proj_kernel(a_ref, w_ref, cos_ref, sin_ref, qw_ref, kw_ref, o_ref, wbf_ref, *, tn, d_model):
    j = pl.program_id(0)

    @pl.when(pl.program_id(1) == 0)
    def _cast():
        wbf_ref[...] = w_ref[...].astype(BF16)

    def matmul():
        return jnp.dot(a_ref[...], wbf_ref[...], preferred_element_type=F32)

    t = d_model // tn
    half = t // 2
    groups = tn // HEAD_DIM

    def rope(y):
        return y * cos_ref[...] + pltpu.roll(y, HEAD_DIM // 2, 1) * sin_ref[...]

    @pl.when(j < t)
    def _ret_qk():
        acc = matmul()
        s = jnp.where(j >= half, HEAD_DIM ** -0.5, 1.0).astype(F32)
        for c in range(groups):
            sl = slice(c * HEAD_DIM, (c + 1) * HEAD_DIM)
            o_ref[:, sl] = (rope(acc[:, sl]) * s).astype(o_ref.dtype)

    @pl.when(((j >= t) & (j < 2 * t)) | ((j >= 5 * t) & (j < 6 * t)))
    def _plain():
        o_ref[...] = matmul().astype(o_ref.dtype)

    @pl.when((j >= 2 * t) & (j < 3 * t))
    def _silu():
        acc = matmul()
        o_ref[...] = (acc * jax.nn.sigmoid(acc)).astype(o_ref.dtype)

    @pl.when((j >= 3 * t) & (j < 5 * t))
    def _diff_qk():
        acc = matmul()
        is_q = (j < 4 * t).astype(F32)
        w = is_q * (qw_ref[...] * (HEAD_DIM ** -0.5 * LOG2E)) + (1.0 - is_q) * kw_ref[...]
        mean_w = jnp.full((2 * HEAD_DIM, HEAD_DIM), 1.0 / HEAD_DIM, BF16)
        for c in range(groups):
            sl = slice(c * HEAD_DIM, (c + 1) * HEAD_DIM)
            y = acc[:, sl]
            sq = y * y
            hi = sq.astype(BF16)
            lo = (sq - hi.astype(F32)).astype(BF16)
            ms = jnp.dot(jnp.concatenate([hi, lo], axis=1), mean_w, preferred_element_type=F32)
            y = y * lax.rsqrt(ms + NORM_EPS) * w
            o_ref[:, sl] = rope(y).astype(o_ref.dtype)

    @pl.when(j >= 6 * t)
    def _sigmoid():
        o_ref[...] = jax.nn.sigmoid(matmul()).astype(o_ref.dtype)


def _in_proj(xn, w_in, cos, sin_s, qw3, kw3, layer, tm, tn):
    L, D = xn.shape
    n_cols = w_in.shape[-1]
    tm = min(tm, L)
    tn = min(tn, D // 2)
    return pl.pallas_call(
        functools.partial(_proj_kernel, tn=tn, d_model=D),
        grid=(n_cols // tn, L // tm),
        in_specs=[pl.BlockSpec((tm, D), lambda j, i: (i, 0)),
                  pl.BlockSpec((None, D, tn), lambda j, i: (layer, 0, j)),
                  pl.BlockSpec((tm, HEAD_DIM), lambda j, i: (i, 0)),
                  pl.BlockSpec((tm, HEAD_DIM), lambda j, i: (i, 0)),
                  pl.BlockSpec((None, 1, HEAD_DIM), lambda j, i: (layer, 0, 0)),
                  pl.BlockSpec((None, 1, HEAD_DIM), lambda j, i: (layer, 0, 0))],
        out_specs=pl.BlockSpec((tm, tn), lambda j, i: (i, j)),
        out_shape=jax.ShapeDtypeStruct((L, n_cols), BF16),
        scratch_shapes=[pltpu.VMEM((D, tn), BF16)],
        compiler_params=_params("arbitrary", "arbitrary"),
        name="in_proj",
    )(xn, w_in, cos, sin_s, qw3, kw3)


def _log_sigmoid(x):
    return jnp.minimum(x, 0.0) - jnp.log1p(jnp.exp(-jnp.abs(x)))


def _retention_kernel(lf_ref, lb_ref, q_ref, k_ref, v_ref, g_ref, gnw_ref, o_ref,
                      state_ref, dmask_ref, wq_ref, wk_ref, *, chunk, n_chunks):
    C, N = chunk, n_chunks
    lgf = _log_sigmoid(lf_ref[...])
    lgb = _log_sigmoid(lb_ref[...])
    lgf1, lgb1 = lgf[:, :1], lgb[:, :1]

    row = lax.broadcasted_iota(jnp.int32, (C, HEAD_DIM), 0).astype(F32)
    wq_ref[0] = jnp.exp((row + 1.0) * lgf)
    wq_ref[1] = jnp.exp((C - row) * lgb)
    wk_ref[0] = jnp.exp((C - 1.0 - row) * lgf)
    wk_ref[1] = jnp.exp(row * lgb)
    r = lax.broadcasted_iota(jnp.int32, (C, C), 0)
    c = lax.broadcasted_iota(jnp.int32, (C, C), 1)
    diff = (r - c).astype(F32)
    dmask_ref[...] = jnp.where(diff > 0, jnp.exp(jnp.maximum(diff, 0.0) * lgf1),
                               jnp.where(diff < 0, jnp.exp(jnp.maximum(-diff, 0.0) * lgb1), 2.0))
    dec_f = jnp.exp(C * lgf1)
    dec_b = jnp.exp(C * lgb1)

    def rows(n):
        return pl.ds(pl.multiple_of(n * C, C), C)

    def states(t, carry):
        sf, sb = carry
        nb = N - 1 - t
        kf = (k_ref[rows(t), :].astype(F32) * wk_ref[0]).astype(BF16)
        kb = (k_ref[rows(nb), :].astype(F32) * wk_ref[1]).astype(BF16)
        kvf = lax.dot_general(kf, v_ref[rows(t), :], _TN, preferred_element_type=F32)
        kvb = lax.dot_general(kb, v_ref[rows(nb), :], _TN, preferred_element_type=F32)
        state_ref[t, 0:HEAD_DIM, :] = sf.astype(BF16)
        state_ref[nb, HEAD_DIM:2 * HEAD_DIM, :] = sb.astype(BF16)
        return sf * dec_f + kvf, sb * dec_b + kvb

    zero = jnp.zeros((HEAD_DIM, V_HEAD_DIM), F32)
    lax.fori_loop(0, N, states, (zero, zero), unroll=min(N, RET_UNROLL))

    def chunk_out(n, carry):
        q = q_ref[rows(n), :]
        s = lax.dot_general(q, k_ref[rows(n), :], _NT, preferred_element_type=F32)
        sd = (s * dmask_ref[...]).astype(BF16)
        qf = q.astype(F32)
        qcat = jnp.concatenate([qf * wq_ref[0], qf * wq_ref[1]], axis=1).astype(BF16)
        o = (jnp.dot(sd, v_ref[rows(n), :], preferred_element_type=F32)
             + jnp.dot(qcat, state_ref[n], preferred_element_type=F32))
        mu = jnp.mean(o, axis=-1, keepdims=True)
        xc = o - mu
        var = jnp.mean(xc * xc, axis=-1, keepdims=True)
        y = xc * lax.rsqrt(var + GN_EPS) * gnw_ref[...]
        o_ref[rows(n), :] = (g_ref[rows(n), :].astype(F32) * y).astype(o_ref.dtype)
        return carry

    lax.fori_loop(0, N, chunk_out, 0, unroll=min(N, RET_UNROLL))


def _retention(proj, lf4, lb4, gnw4, layer, d_model):
    L = proj.shape[0]
    H = d_model // V_HEAD_DIM
    C = min(RET_CHUNK, L)
    N = L // C
    qk_blocks = d_model // 2 // HEAD_DIM
    v_blocks = d_model // V_HEAD_DIM
    return pl.pallas_call(
        functools.partial(_retention_kernel, chunk=C, n_chunks=N),
        grid=(H,),
        in_specs=[pl.BlockSpec((None, None, 1, HEAD_DIM), lambda h: (layer, h, 0, 0)),
                  pl.BlockSpec((None, None, 1, HEAD_DIM), lambda h: (layer, h, 0, 0)),
                  pl.BlockSpec((L, HEAD_DIM), lambda h: (0, h)),
                  pl.BlockSpec((L, HEAD_DIM), lambda h: (0, qk_blocks + h)),
                  pl.BlockSpec((L, V_HEAD_DIM), lambda h: (0, v_blocks + h)),
                  pl.BlockSpec((L, V_HEAD_DIM), lambda h: (0, 2 * v_blocks + h)),
                  pl.BlockSpec((None, None, 1, V_HEAD_DIM), lambda h: (layer, h, 0, 0))],
        out_specs=pl.BlockSpec((L, V_HEAD_DIM), lambda h: (0, h)),
        out_shape=jax.ShapeDtypeStruct((L, d_model), BF16),
        scratch_shapes=[pltpu.VMEM((N, 2 * HEAD_DIM, V_HEAD_DIM), BF16),
                        pltpu.VMEM((C, C), F32),
                        pltpu.VMEM((2, C, HEAD_DIM), F32),
                        pltpu.VMEM((2, C, HEAD_DIM), F32)],
        compiler_params=_params("arbitrary"),
        name="retention",
    )(lf4, lb4, proj, proj, proj, proj, gnw4)


def _attn_kernel(lam_ref, qw_ref, kw_ref, q_ref, k_ref, v_ref, w_ref, o_ref, acc_ref, l_ref,
                 *, tk, n_kv, lambda_init):
    tq = q_ref.shape[0]
    q = q_ref[...]
    qs = (q[:, :HEAD_DIM], q[:, HEAD_DIM:])
    score_bound = (HEAD_DIM * (HEAD_DIM ** -0.5 * LOG2E) * SCORE_MARGIN
                   * jnp.max(jnp.abs(qw_ref[...])) * jnp.max(jnp.abs(kw_ref[...])))

    def scores(t, k):
        return lax.dot_general(qs[t], k[:, t * HEAD_DIM:(t + 1) * HEAD_DIM], _NT, preferred_element_type=F32)

    def finish(l1, l2):
        lp = lam_ref[...]
        lam = (jnp.exp(jnp.sum(lp[0:1] * lp[1:2], axis=-1, keepdims=True))
               - jnp.exp(jnp.sum(lp[2:3] * lp[3:4], axis=-1, keepdims=True)) + lambda_init)
        o = acc_ref[0] * (1.0 / l1) - lam * (acc_ref[1] * (1.0 / l2))
        ms = jnp.mean(o * o, axis=-1, keepdims=True)
        y = o * lax.rsqrt(ms + NORM_EPS) * w_ref[...] * (1.0 - lambda_init)
        o_ref[...] = y.astype(o_ref.dtype)

    @pl.when(score_bound <= MAX_EXP2_ARG)
    def _bounded():
        for j in range(n_kv):
            k = k_ref[j * tk:(j + 1) * tk, :]
            v = v_ref[j * tk:(j + 1) * tk, :]
            for t in range(2):
                p = jnp.exp2(scores(t, k))
                lsum = p[:, :HEAD_DIM]
                for c in range(1, tk // HEAD_DIM):
                    lsum = lsum + p[:, c * HEAD_DIM:(c + 1) * HEAD_DIM]
                pv = jnp.dot(p.astype(BF16), v, preferred_element_type=F32)
                if j == 0:
                    l_ref[t] = lsum
                    acc_ref[t] = pv
                else:
                    l_ref[t] += lsum
                    acc_ref[t] += pv
        finish(jnp.sum(l_ref[0], axis=-1, keepdims=True), jnp.sum(l_ref[1], axis=-1, keepdims=True))

    @pl.when(score_bound > MAX_EXP2_ARG)
    def _online():
        acc_ref[...] = jnp.zeros_like(acc_ref)

        def body(j, carry):
            rows = pl.ds(pl.multiple_of(j * tk, tk), tk)
            k = k_ref[rows, :]
            v = v_ref[rows, :]
            out = []
            for t in range(2):
                m, l = carry[2 * t], carry[2 * t + 1]
                s = scores(t, k)
                m_new = jnp.maximum(m, jnp.max(s, axis=-1, keepdims=True))
                alpha = jnp.exp2(m - m_new)
                p = jnp.exp2(s - m_new)
                l_new = alpha * l + jnp.sum(p, axis=-1, keepdims=True)
                acc_ref[t] = alpha * acc_ref[t] + jnp.dot(p.astype(BF16), v, preferred_element_type=F32)
                out += [m_new, l_new]
            return tuple(out)

        m0 = jnp.full((tq, 1), NEG_BIG, F32)
        l0 = jnp.zeros((tq, 1), F32)
        _, l1, _, l2 = lax.fori_loop(0, n_kv, body, (m0, l0, m0, l0))
        finish(l1, l2)


def _diff_attention(proj, diff_lambda, qw3, kw3, subln3, layer, d_model, tq, tk):
    L = proj.shape[0]
    H = d_model // V_HEAD_DIM
    tq = min(tq, L)
    tk = min(tk, L)
    vb = d_model // V_HEAD_DIM
    lambda_init = 0.8 - 0.6 * math.exp(-0.3 * layer)
    return pl.pallas_call(
        functools.partial(_attn_kernel, tk=tk, n_kv=L // tk, lambda_init=lambda_init),
        grid=(H, L // tq),
        in_specs=[pl.BlockSpec((None, 4, HEAD_DIM), lambda h, i: (layer, 0, 0)),
                  pl.BlockSpec((None, 1, HEAD_DIM), lambda h, i: (layer, 0, 0)),
                  pl.BlockSpec((None, 1, HEAD_DIM), lambda h, i: (layer, 0, 0)),
                  pl.BlockSpec((tq, V_HEAD_DIM), lambda h, i: (i, 3 * vb + h)),
                  pl.BlockSpec((L, V_HEAD_DIM), lambda h, i: (0, 4 * vb + h)),
                  pl.BlockSpec((L, V_HEAD_DIM), lambda h, i: (0, 5 * vb + h)),
                  pl.BlockSpec((None, 1, V_HEAD_DIM), lambda h, i: (layer, 0, 0))],
        out_specs=pl.BlockSpec((tq, V_HEAD_DIM), lambda h, i: (i, h)),
        out_shape=jax.ShapeDtypeStruct((L, d_model), BF16),
        scratch_shapes=[pltpu.VMEM((2, tq, V_HEAD_DIM), F32),
                        pltpu.VMEM((2, tq, HEAD_DIM), F32)],
        compiler_params=_params("arbitrary", "arbitrary"),
        name="diff_attention",
    )(diff_lambda, qw3, kw3, proj, proj, proj, subln3)


def _merge_kernel(a1_ref, a2_ref, w1_ref, w2_ref, g1_ref, g2_ref, o_ref, w1bf_ref, w2bf_ref):
    @pl.when(pl.program_id(1) == 0)
    def _cast():
        w1bf_ref[...] = w1_ref[...].astype(BF16)
        w2bf_ref[...] = w2_ref[...].astype(BF16)

    y1 = jnp.dot(a1_ref[...], w1bf_ref[...], preferred_element_type=F32)
    y2 = jnp.dot(a2_ref[...], w2bf_ref[...], preferred_element_type=F32)
    o_ref[...] = (g1_ref[...].astype(F32) * y1 + g2_ref[...].astype(F32) * y2).astype(o_ref.dtype)


def _merge(ret, att, w_ret_out, w_diff_out, proj, layer, tm, tn):
    L, D = ret.shape
    tm = min(tm, L)
    tn = min(tn, D)
    gb = D // tn
    return pl.pallas_call(
        _merge_kernel,
        grid=(D // tn, L // tm),
        in_specs=[pl.BlockSpec((tm, D), lambda j, i: (i, 0)),
                  pl.BlockSpec((tm, D), lambda j, i: (i, 0)),
                  pl.BlockSpec((None, D, tn), lambda j, i: (layer, 0, j)),
                  pl.BlockSpec((None, D, tn), lambda j, i: (layer, 0, j)),
                  pl.BlockSpec((tm, tn), lambda j, i: (i, 6 * gb + j)),
                  pl.BlockSpec((tm, tn), lambda j, i: (i, 7 * gb + j))],
        out_specs=pl.BlockSpec((tm, tn), lambda j, i: (i, j)),
        out_shape=jax.ShapeDtypeStruct((L, D), BF16),
        scratch_shapes=[pltpu.VMEM((D, tn), BF16), pltpu.VMEM((D, tn), BF16)],
        compiler_params=_params("arbitrary", "arbitrary"),
        name="merge",
    )(ret, att, w_ret_out, w_diff_out, proj, proj)


def _matmul_kernel(a_ref, w_ref, *rest, residual, relu2):
    if residual:
        x_ref, o_ref, wbf_ref = rest
    else:
        o_ref, wbf_ref = rest

    @pl.when(pl.program_id(1) == 0)
    def _cast():
        wbf_ref[...] = w_ref[...].astype(BF16)

    y = jnp.dot(a_ref[...], wbf_ref[...], preferred_element_type=F32)
    if relu2:
        y = jnp.square(jnp.maximum(y, 0.0))
    if residual:
        y = x_ref[...] + y
    o_ref[...] = y.astype(o_ref.dtype)


def _matmul(a, w, layer, tm, tn, *, x=None, relu2=False, out_dtype=F32, name):
    L, K = a.shape
    n_cols = w.shape[-1]
    tm = min(tm, L)
    tn = min(tn, n_cols)
    in_specs = [pl.BlockSpec((tm, K), lambda j, i: (i, 0)),
                pl.BlockSpec((None, K, tn), lambda j, i: (layer, 0, j))]
    args = [a, w]
    if x is not None:
        in_specs.append(pl.BlockSpec((tm, tn), lambda j, i: (i, j)))
        args.append(x)
    return pl.pallas_call(
        functools.partial(_matmul_kernel, residual=x is not None, relu2=relu2),
        grid=(n_cols // tn, L // tm),
        in_specs=in_specs,
        out_specs=pl.BlockSpec((tm, tn), lambda j, i: (i, j)),
        out_shape=jax.ShapeDtypeStruct((L, n_cols), out_dtype),
        scratch_shapes=[pltpu.VMEM((K, tn), BF16)],
        compiler_params=_params("arbitrary", "arbitrary"),
        name=name,
    )(*args)


def _matmul_kgrid_kernel(a_ref, w_ref, x_ref, o_ref):
    @pl.when(pl.program_id(2) == 0)
    def _init():
        o_ref[...] = x_ref[...]

    o_ref[...] += jnp.dot(a_ref[...], w_ref[...].astype(BF16), preferred_element_type=F32)


def _matmul_kgrid(a, w, layer, x, tm, tn, tk, *, name):
    L, K = a.shape
    n_cols = w.shape[-1]
    tm, tn, tk = min(tm, L), min(tn, n_cols), min(tk, K)
    return pl.pallas_call(
        _matmul_kgrid_kernel,
        grid=(n_cols // tn, L // tm, K // tk),
        in_specs=[pl.BlockSpec((tm, tk), lambda j, i, k: (i, k)),
                  pl.BlockSpec((None, tk, tn), lambda j, i, k: (layer, k, j)),
                  pl.BlockSpec((tm, tn), lambda j, i, k: (i, j))],
        out_specs=pl.BlockSpec((tm, tn), lambda j, i, k: (i, j)),
        out_shape=jax.ShapeDtypeStruct((L, n_cols), F32),
        compiler_params=_params("arbitrary", "arbitrary", "arbitrary"),
        name=name,
    )(a, w, x)


def _rope_tables(L):
    inv = 1.0 / (ROPE_THETA ** (jnp.arange(0, HEAD_DIM, 2, dtype=F32) / HEAD_DIM))
    ang = jnp.arange(L, dtype=F32)[:, None] * inv[None, :]
    ang = jnp.concatenate([ang, ang], axis=-1)
    sign = jnp.where(jnp.arange(HEAD_DIM) < HEAD_DIM // 2, -1.0, 1.0).astype(F32)
    return jnp.cos(ang), jnp.sin(ang) * sign[None, :]


def kernel(x, norm_mix_w, w_in, ret_decay_fwd, ret_decay_bwd, ret_gn_w, w_ret_out, q_norm_w, k_norm_w,
           diff_lambda, diff_subln_w, w_diff_out, w_out, norm_mlp_w, w_mlp_in, w_mlp_out):
    B, L, D = x.shape
    assert B == 1 and D % V_HEAD_DIM == 0
    depth = w_in.shape[0]
    H = D // V_HEAD_DIM
    cos, sin_s = _rope_tables(L)

    norm_mix3 = norm_mix_w.reshape(depth, 1, D)
    norm_mlp3 = norm_mlp_w.reshape(depth, 1, D)
    qw3 = q_norm_w.reshape(depth, 1, HEAD_DIM)
    kw3 = k_norm_w.reshape(depth, 1, HEAD_DIM)
    subln3 = diff_subln_w.reshape(depth, 1, V_HEAD_DIM)
    gnw4 = ret_gn_w.reshape(depth, H, 1, V_HEAD_DIM)
    lf4 = jnp.broadcast_to(ret_decay_fwd[:, :, None, None], (depth, H, 1, HEAD_DIM))
    lb4 = jnp.broadcast_to(ret_decay_bwd[:, :, None, None], (depth, H, 1, HEAD_DIM))

    h = x.reshape(L, D)
    for layer in range(depth):
        xn = _rmsnorm(h, norm_mix3, layer, 512)
        proj = _in_proj(xn, w_in, cos, sin_s, qw3, kw3, layer, 1024, 1024)
        ret = _retention(proj, lf4, lb4, gnw4, layer, D)
        att = _diff_attention(proj, diff_lambda, qw3, kw3, subln3, layer, D, 1024, 1024)
        merged = _merge(ret, att, w_ret_out, w_diff_out, proj, layer, 1024, 512)
        h = _matmul(merged, w_out, layer, 1024, 1024, x=h, name="out_proj")
        hn = _rmsnorm(h, norm_mlp3, layer, 512)
        u = _matmul(hn, w_mlp_in, layer, 1024, 1024, relu2=True, out_dtype=BF16, name="mlp_in")
        h = _matmul_kgrid(u, w_mlp_out, layer, h, 2048, 1024, 1024, name="mlp_out")
    return h.reshape(B, L, D)
```

```python
import functools
import math

import jax
import jax.numpy as jnp
from jax import lax
from jax.experimental import pallas as pl
from jax.experimental.pallas import tpu as pltpu

HEAD_DIM = 128
V_HEAD_DIM = 2 * HEAD_DIM
NORM_EPS = 1e-6
GN_EPS = 1e-5
ROPE_THETA = 10000.0
LOG2E = 1.4426950408889634
RET_CHUNK = 256
RET_UNROLL = 32
NEG_BIG = -1e30
MAX_EXP2_ARG = 50.0
SCORE_MARGIN = 1.01
VMEM_LIMIT_V7X = 56 * 1024 * 1024

F32 = jnp.float32
BF16 = jnp.bfloat16
_NT = (((1,), (1,)), ((), ()))
_TN = (((0,), (0,)), ((), ()))


def _params(*sem):
    return pltpu.CompilerParams(dimension_semantics=sem, vmem_limit_bytes=VMEM_LIMIT_V7X)


def _rmsnorm_kernel(x_ref, w_ref, o_ref):
    x = x_ref[...]
    ms = jnp.mean(x * x, axis=-1, keepdims=True)
    o_ref[...] = (x * lax.rsqrt(ms + NORM_EPS) * w_ref[...]).astype(o_ref.dtype)


def _rmsnorm(x, w3, layer, tm):
    L, D = x.shape
    tm = min(tm, L)
    return pl.pallas_call(
        _rmsnorm_kernel,
        grid=(L // tm,),
        in_specs=[pl.BlockSpec((tm, D), lambda i: (i, 0)),
                  pl.BlockSpec((None, 1, D), lambda i: (layer, 0, 0))],
        out_specs=pl.BlockSpec((tm, D), lambda i: (i, 0)),
        out_shape=jax.ShapeDtypeStruct((L, D), BF16),
        compiler_params=_params("arbitrary"),
        name="rmsnorm",
    )(x, w3)


def _logistic(x):
    return 0.5 * jnp.tanh(0.5 * x) + 0.5


def _proj_kernel(a_ref, w_ref, cos_ref, sin_ref, qw_ref, kw_ref, o_ref, wbf_ref, *, tn, d_model):
    j = pl.program_id(0)

    @pl.when(pl.program_id(1) == 0)
    def _cast():
        wbf_ref[...] = w_ref[...].astype(BF16)

    def matmul():
        return jnp.dot(a_ref[...], wbf_ref[...], preferred_element_type=F32)

    t = d_model // tn
    half = t // 2
    groups = tn // HEAD_DIM

    def rope(y):
        return y * cos_ref[...] + pltpu.roll(y, HEAD_DIM // 2, 1) * sin_ref[...]

    @pl.when(j < t)
    def _ret_qk():
        acc = matmul()
        s = jnp.where(j >= half, HEAD_DIM ** -0.5, 1.0).astype(F32)
        for c in range(groups):
            sl = slice(c * HEAD_DIM, (c + 1) * HEAD_DIM)
            o_ref[:, sl] = (rope(acc[:, sl]) * s).astype(o_ref.dtype)

    @pl.when(((j >= t) & (j < 2 * t)) | ((j >= 5 * t) & (j < 6 * t)))
    def _plain():
        o_ref[...] = matmul().astype(o_ref.dtype)

    @pl.when((j >= 2 * t) & (j < 3 * t))
    def _silu():
        acc = matmul()
        o_ref[...] = (acc * _logistic(acc)).astype(o_ref.dtype)

    @pl.when((j >= 3 * t) & (j < 5 * t))
    def _diff_qk():
        acc = matmul()
        is_q = (j < 4 * t).astype(F32)
        w = is_q * (qw_ref[...] * (HEAD_DIM ** -0.5 * LOG2E)) + (1.0 - is_q) * kw_ref[...]
        mean_w = jnp.full((2 * HEAD_DIM, HEAD_DIM), 1.0 / HEAD_DIM, BF16)
        for c in range(groups):
            sl = slice(c * HEAD_DIM, (c + 1) * HEAD_DIM)
            y = acc[:, sl]
            sq = y * y
            hi = sq.astype(BF16)
            lo = (sq - hi.astype(F32)).astype(BF16)
            ms = jnp.dot(jnp.concatenate([hi, lo], axis=1), mean_w, preferred_element_type=F32)
            y = y * lax.rsqrt(ms + NORM_EPS) * w
            o_ref[:, sl] = rope(y).astype(o_ref.dtype)

    @pl.when(j >= 6 * t)
    def _sigmoid():
        o_ref[...] = _logistic(matmul()).astype(o_ref.dtype)


def _in_proj(xn, w_in, cos, sin_s, qw3, kw3, layer, tm, tn):
    L, D = xn.shape
    n_cols = w_in.shape[-1]
    tm = min(tm, L)
    tn = min(tn, D // 2)
    return pl.pallas_call(
        functools.partial(_proj_kernel, tn=tn, d_model=D),
        grid=(n_cols // tn, L // tm),
        in_specs=[pl.BlockSpec((tm, D), lambda j, i: (i, 0)),
                  pl.BlockSpec((None, D, tn), lambda j, i: (layer, 0, j)),
                  pl.BlockSpec((tm, HEAD_DIM), lambda j, i: (i, 0)),
                  pl.BlockSpec((tm, HEAD_DIM), lambda j, i: (i, 0)),
                  pl.BlockSpec((None, 1, HEAD_DIM), lambda j, i: (layer, 0, 0)),
                  pl.BlockSpec((None, 1, HEAD_DIM), lambda j, i: (layer, 0, 0))],
        out_specs=pl.BlockSpec((tm, tn), lambda j, i: (i, j)),
        out_shape=jax.ShapeDtypeStruct((L, n_cols), BF16),
        scratch_shapes=[pltpu.VMEM((D, tn), BF16)],
        compiler_params=_params("arbitrary", "arbitrary"),
        name="in_proj",
    )(xn, w_in, cos, sin_s, qw3, kw3)


def _log_sigmoid(x):
    return jnp.minimum(x, 0.0) - jnp.log1p(jnp.exp(-jnp.abs(x)))


def _retention_kernel(lf_ref, lb_ref, q_ref, k_ref, v_ref, g_ref, gnw_ref, o_ref,
                      state_ref, dmask_ref, wq_ref, wk_ref, *, chunk, n_chunks):
    C, N = chunk, n_chunks
    lgf = _log_sigmoid(lf_ref[...])
    lgb = _log_sigmoid(lb_ref[...])
    lgf1, lgb1 = lgf[:, :1], lgb[:, :1]

    row = lax.broadcasted_iota(jnp.int32, (C, HEAD_DIM), 0).astype(F32)
    wq_ref[0] = jnp.exp((row + 1.0) * lgf)
    wq_ref[1] = jnp.exp((C - row) * lgb)
    wk_ref[0] = jnp.exp((C - 1.0 - row) * lgf)
    wk_ref[1] = jnp.exp(row * lgb)
    r = lax.broadcasted_iota(jnp.int32, (C, C), 0)
    c = lax.broadcasted_iota(jnp.int32, (C, C), 1)
    diff = (r - c).astype(F32)
    dmask_ref[...] = jnp.where(diff > 0, jnp.exp(jnp.maximum(diff, 0.0) * lgf1),
                               jnp.where(diff < 0, jnp.exp(jnp.maximum(-diff, 0.0) * lgb1), 2.0))
    dec_f = jnp.exp(C * lgf1)
    dec_b = jnp.exp(C * lgb1)

    def rows(n):
        return pl.ds(pl.multiple_of(n * C, C), C)

    def states(t, carry):
        sf, sb = carry
        nb = N - 1 - t
        kf = (k_ref[rows(t), :].astype(F32) * wk_ref[0]).astype(BF16)
        kb = (k_ref[rows(nb), :].astype(F32) * wk_ref[1]).astype(BF16)
        kvf = lax.dot_general(kf, v_ref[rows(t), :], _TN, preferred_element_type=F32)
        kvb = lax.dot_general(kb, v_ref[rows(nb), :], _TN, preferred_element_type=F32)
        state_ref[t, 0:HEAD_DIM, :] = sf.astype(BF16)
        state_ref[nb, HEAD_DIM:2 * HEAD_DIM, :] = sb.astype(BF16)
        return sf * dec_f + kvf, sb * dec_b + kvb

    zero = jnp.zeros((HEAD_DIM, V_HEAD_DIM), F32)
    lax.fori_loop(0, N, states, (zero, zero), unroll=min(N, RET_UNROLL))

    def chunk_out(n, carry):
        q = q_ref[rows(n), :]
        s = lax.dot_general(q, k_ref[rows(n), :], _NT, preferred_element_type=F32)
        sd = (s * dmask_ref[...]).astype(BF16)
        qf = q.astype(F32)
        qcat = jnp.concatenate([qf * wq_ref[0], qf * wq_ref[1]], axis=1).astype(BF16)
        o = (jnp.dot(sd, v_ref[rows(n), :], preferred_element_type=F32)
             + jnp.dot(qcat, state_ref[n], preferred_element_type=F32))
        mu = jnp.mean(o, axis=-1, keepdims=True)
        xc = o - mu
        var = jnp.mean(xc * xc, axis=-1, keepdims=True)
        y = xc * lax.rsqrt(var + GN_EPS) * gnw_ref[...]
        o_ref[rows(n), :] = (g_ref[rows(n), :].astype(F32) * y).astype(o_ref.dtype)
        return carry

    lax.fori_loop(0, N, chunk_out, 0, unroll=min(N, RET_UNROLL))


def _retention(proj, lf4, lb4, gnw4, layer, d_model):
    L = proj.shape[0]
    H = d_model // V_HEAD_DIM
    C = min(RET_CHUNK, L)
    N = L // C
    qk_blocks = d_model // 2 // HEAD_DIM
    v_blocks = d_model // V_HEAD_DIM
    return pl.pallas_call(
        functools.partial(_retention_kernel, chunk=C, n_chunks=N),
        grid=(H,),
        in_specs=[pl.BlockSpec((None, None, 1, HEAD_DIM), lambda h: (layer, h, 0, 0)),
                  pl.BlockSpec((None, None, 1, HEAD_DIM), lambda h: (layer, h, 0, 0)),
                  pl.BlockSpec((L, HEAD_DIM), lambda h: (0, h)),
                  pl.BlockSpec((L, HEAD_DIM), lambda h: (0, qk_blocks + h)),
                  pl.BlockSpec((L, V_HEAD_DIM), lambda h: (0, v_blocks + h)),
                  pl.BlockSpec((L, V_HEAD_DIM), lambda h: (0, 2 * v_blocks + h)),
                  pl.BlockSpec((None, None, 1, V_HEAD_DIM), lambda h: (layer, h, 0, 0))],
        out_specs=pl.BlockSpec((L, V_HEAD_DIM), lambda h: (0, h)),
        out_shape=jax.ShapeDtypeStruct((L, d_model), BF16),
        scratch_shapes=[pltpu.VMEM((N, 2 * HEAD_DIM, V_HEAD_DIM), BF16),
                        pltpu.VMEM((C, C), F32),
                        pltpu.VMEM((2, C, HEAD_DIM), F32),
                        pltpu.VMEM((2, C, HEAD_DIM), F32)],
        compiler_params=_params("arbitrary"),
        name="retention",
    )(lf4, lb4, proj, proj, proj, proj, gnw4)


def _attn_kernel(lam_ref, qw_ref, kw_ref, q_ref, k_ref, v_ref, w_ref, o_ref, acc_ref, l_ref,
                 *, tk, n_kv, lambda_init):
    tq = q_ref.shape[0]
    q = q_ref[...]
    qs = (q[:, :HEAD_DIM], q[:, HEAD_DIM:])
    score_bound = (HEAD_DIM * (HEAD_DIM ** -0.5 * LOG2E) * SCORE_MARGIN
                   * jnp.max(jnp.abs(qw_ref[...])) * jnp.max(jnp.abs(kw_ref[...])))

    def scores(t, k):
        return lax.dot_general(qs[t], k[:, t * HEAD_DIM:(t + 1) * HEAD_DIM], _NT, preferred_element_type=F32)

    def finish(l1, l2):
        lp = lam_ref[...]
        lam = (jnp.exp(jnp.sum(lp[0:1] * lp[1:2], axis=-1, keepdims=True))
               - jnp.exp(jnp.sum(lp[2:3] * lp[3:4], axis=-1, keepdims=True)) + lambda_init)
        o = acc_ref[0] * (1.0 / l1) - lam * (acc_ref[1] * (1.0 / l2))
        ms = jnp.mean(o * o, axis=-1, keepdims=True)
        y = o * lax.rsqrt(ms + NORM_EPS) * w_ref[...] * (1.0 - lambda_init)
        o_ref[...] = y.astype(o_ref.dtype)

    @pl.when(score_bound <= MAX_EXP2_ARG)
    def _bounded():
        for j in range(n_kv):
            k = k_ref[j * tk:(j + 1) * tk, :]
            v = v_ref[j * tk:(j + 1) * tk, :]
            for t in range(2):
                p = jnp.exp2(scores(t, k))
                lsum = p[:, :HEAD_DIM]
                for c in range(1, tk // HEAD_DIM):
                    lsum = lsum + p[:, c * HEAD_DIM:(c + 1) * HEAD_DIM]
                pv = jnp.dot(p.astype(BF16), v, preferred_element_type=F32)
                if j == 0:
                    l_ref[t] = lsum
                    acc_ref[t] = pv
                else:
                    l_ref[t] += lsum
                    acc_ref[t] += pv
        finish(jnp.sum(l_ref[0], axis=-1, keepdims=True), jnp.sum(l_ref[1], axis=-1, keepdims=True))

    @pl.when(score_bound > MAX_EXP2_ARG)
    def _online():
        acc_ref[...] = jnp.zeros_like(acc_ref)

        def body(j, carry):
            rows = pl.ds(pl.multiple_of(j * tk, tk), tk)
            k = k_ref[rows, :]
            v = v_ref[rows, :]
            out = []
            for t in range(2):
                m, l = carry[2 * t], carry[2 * t + 1]
                s = scores(t, k)
                m_new = jnp.maximum(m, jnp.max(s, axis=-1, keepdims=True))
                alpha = jnp.exp2(m - m_new)
                p = jnp.exp2(s - m_new)
                l_new = alpha * l + jnp.sum(p, axis=-1, keepdims=True)
                acc_ref[t] = alpha * acc_ref[t] + jnp.dot(p.astype(BF16), v, preferred_element_type=F32)
                out += [m_new, l_new]
            return tuple(out)

        m0 = jnp.full((tq, 1), NEG_BIG, F32)
        l0 = jnp.zeros((tq, 1), F32)
        _, l1, _, l2 = lax.fori_loop(0, n_kv, body, (m0, l0, m0, l0))
        finish(l1, l2)


def _diff_attention(proj, diff_lambda, qw3, kw3, subln3, layer, d_model, tq, tk):
    L = proj.shape[0]
    H = d_model // V_HEAD_DIM
    tq = min(tq, L)
    tk = min(tk, L)
    vb = d_model // V_HEAD_DIM
    lambda_init = 0.8 - 0.6 * math.exp(-0.3 * layer)
    return pl.pallas_call(
        functools.partial(_attn_kernel, tk=tk, n_kv=L // tk, lambda_init=lambda_init),
        grid=(H, L // tq),
        in_specs=[pl.BlockSpec((None, 4, HEAD_DIM), lambda h, i: (layer, 0, 0)),
                  pl.BlockSpec((None, 1, HEAD_DIM), lambda h, i: (layer, 0, 0)),
                  pl.BlockSpec((None, 1, HEAD_DIM), lambda h, i: (layer, 0, 0)),
                  pl.BlockSpec((tq, V_HEAD_DIM), lambda h, i: (i, 3 * vb + h)),
                  pl.BlockSpec((L, V_HEAD_DIM), lambda h, i: (0, 4 * vb + h)),
                  pl.BlockSpec((L, V_HEAD_DIM), lambda h, i: (0, 5 * vb + h)),
                  pl.BlockSpec((None, 1, V_HEAD_DIM), lambda h, i: (layer, 0, 0))],
        out_specs=pl.BlockSpec((tq, V_HEAD_DIM), lambda h, i: (i, h)),
        out_shape=jax.ShapeDtypeStruct((L, d_model), BF16),
        scratch_shapes=[pltpu.VMEM((2, tq, V_HEAD_DIM), F32),
                        pltpu.VMEM((2, tq, HEAD_DIM), F32)],
        compiler_params=_params("arbitrary", "arbitrary"),
        name="diff_attention",
    )(diff_lambda, qw3, kw3, proj, proj, proj, subln3)


def _merge_kernel(a1_ref, a2_ref, w1_ref, w2_ref, g1_ref, g2_ref, o_ref, w1bf_ref, w2bf_ref):
    @pl.when(pl.program_id(1) == 0)
    def _cast():
        w1bf_ref[...] = w1_ref[...].astype(BF16)
        w2bf_ref[...] = w2_ref[...].astype(BF16)

    y1 = jnp.dot(a1_ref[...], w1bf_ref[...], preferred_element_type=F32)
    y2 = jnp.dot(a2_ref[...], w2bf_ref[...], preferred_element_type=F32)
    o_ref[...] = (g1_ref[...].astype(F32) * y1 + g2_ref[...].astype(F32) * y2).astype(o_ref.dtype)


def _merge(ret, att, w_ret_out, w_diff_out, proj, layer, tm, tn):
    L, D = ret.shape
    tm = min(tm, L)
    tn = min(tn, D)
    gb = D // tn
    return pl.pallas_call(
        _merge_kernel,
        grid=(D // tn, L // tm),
        in_specs=[pl.BlockSpec((tm, D), lambda j, i: (i, 0)),
                  pl.BlockSpec((tm, D), lambda j, i: (i, 0)),
                  pl.BlockSpec((None, D, tn), lambda j, i: (layer, 0, j)),
                  pl.BlockSpec((None, D, tn), lambda j, i: (layer, 0, j)),
                  pl.BlockSpec((tm, tn), lambda j, i: (i, 6 * gb + j)),
                  pl.BlockSpec((tm, tn), lambda j, i: (i, 7 * gb + j))],
        out_specs=pl.BlockSpec((tm, tn), lambda j, i: (i, j)),
        out_shape=jax.ShapeDtypeStruct((L, D), BF16),
        scratch_shapes=[pltpu.VMEM((D, tn), BF16), pltpu.VMEM((D, tn), BF16)],
        compiler_params=_params("arbitrary", "arbitrary"),
        name="merge",
    )(ret, att, w_ret_out, w_diff_out, proj, proj)


def _matmul_kernel(a_ref, w_ref, *rest, residual, relu2):
    if residual:
        x_ref, o_ref, wbf_ref = rest
    else:
        o_ref, wbf_ref = rest

    @pl.when(pl.program_id(1) == 0)
    def _cast():
        wbf_ref[...] = w_ref[...].astype(BF16)

    y = jnp.dot(a_ref[...], wbf_ref[...], preferred_element_type=F32)
    if relu2:
        y = jnp.square(jnp.maximum(y, 0.0))
    if residual:
        y = x_ref[...] + y
    o_ref[...] = y.astype(o_ref.dtype)


def _matmul(a, w, layer, tm, tn, *, x=None, relu2=False, out_dtype=F32, name):
    L, K = a.shape
    n_cols = w.shape[-1]
    tm = min(tm, L)
    tn = min(tn, n_cols)
    in_specs = [pl.BlockSpec((tm, K), lambda j, i: (i, 0)),
                pl.BlockSpec((None, K, tn), lambda j, i: (layer, 0, j))]
    args = [a, w]
    if x is not None:
        in_specs.append(pl.BlockSpec((tm, tn), lambda j, i: (i, j)))
        args.append(x)
    return pl.pallas_call(
        functools.partial(_matmul_kernel, residual=x is not None, relu2=relu2),
        grid=(n_cols // tn, L // tm),
        in_specs=in_specs,
        out_specs=pl.BlockSpec((tm, tn), lambda j, i: (i, j)),
        out_shape=jax.ShapeDtypeStruct((L, n_cols), out_dtype),
        scratch_shapes=[pltpu.VMEM((K, tn), BF16)],
        compiler_params=_params("arbitrary", "arbitrary"),
        name=name,
    )(*args)


def _matmul_kgrid_kernel(a_ref, w_ref, x_ref, o_ref):
    @pl.when(pl.program_id(2) == 0)
    def _init():
        o_ref[...] = x_ref[...]

    o_ref[...] += jnp.dot(a_ref[...], w_ref[...].astype(BF16), preferred_element_type=F32)


def _matmul_kgrid(a, w, layer, x, tm, tn, tk, *, name):
    L, K = a.shape
    n_cols = w.shape[-1]
    tm, tn, tk = min(tm, L), min(tn, n_cols), min(tk, K)
    return pl.pallas_call(
        _matmul_kgrid_kernel,
        grid=(n_cols // tn, L // tm, K // tk),
        in_specs=[pl.BlockSpec((tm, tk), lambda j, i, k: (i, k)),
                  pl.BlockSpec((None, tk, tn), lambda j, i, k: (layer, k, j)),
                  pl.BlockSpec((tm, tn), lambda j, i, k: (i, j))],
        out_specs=pl.BlockSpec((tm, tn), lambda j, i, k: (i, j)),
        out_shape=jax.ShapeDtypeStruct((L, n_cols), F32),
        compiler_params=_params("arbitrary", "arbitrary", "arbitrary"),
        name=name,
    )(a, w, x)


def _rope_tables(L):
    inv = 1.0 / (ROPE_THETA ** (jnp.arange(0, HEAD_DIM, 2, dtype=F32) / HEAD_DIM))
    ang = jnp.arange(L, dtype=F32)[:, None] * inv[None, :]
    ang = jnp.concatenate([ang, ang], axis=-1)
    sign = jnp.where(jnp.arange(HEAD_DIM) < HEAD_DIM // 2, -1.0, 1.0).astype(F32)
    return jnp.cos(ang), jnp.sin(ang) * sign[None, :]


def kernel(x, norm_mix_w, w_in, ret_decay_fwd, ret_decay_bwd, ret_gn_w, w_ret_out, q_norm_w, k_norm_w,
           diff_lambda, diff_subln_w, w_diff_out, w_out, norm_mlp_w, w_mlp_in, w_mlp_out):
    B, L, D = x.shape
    assert B == 1 and D % V_HEAD_DIM == 0
    depth = w_in.shape[0]
    H = D // V_HEAD_DIM
    cos, sin_s = _rope_tables(L)

    norm_mix3 = norm_mix_w.reshape(depth, 1, D)
    norm_mlp3 = norm_mlp_w.reshape(depth, 1, D)
    qw3 = q_norm_w.reshape(depth, 1, HEAD_DIM)
    kw3 = k_norm_w.reshape(depth, 1, HEAD_DIM)
    subln3 = diff_subln_w.reshape(depth, 1, V_HEAD_DIM)
    gnw4 = ret_gn_w.reshape(depth, H, 1, V_HEAD_DIM)
    lf4 = jnp.broadcast_to(ret_decay_fwd[:, :, None, None], (depth, H, 1, HEAD_DIM))
    lb4 = jnp.broadcast_to(ret_decay_bwd[:, :, None, None], (depth, H, 1, HEAD_DIM))

    h = x.reshape(L, D)
    for layer in range(depth):
        xn = _rmsnorm(h, norm_mix3, layer, 512)
        proj = _in_proj(xn, w_in, cos, sin_s, qw3, kw3, layer, 1024, 1024)
        ret = _retention(proj, lf4, lb4, gnw4, layer, D)
        att = _diff_attention(proj, diff_lambda, qw3, kw3, subln3, layer, D, 1024, 1024)
        merged = _merge(ret, att, w_ret_out, w_diff_out, proj, layer, 1024, 512)
        h = _matmul(merged, w_out, layer, 1024, 1024, x=h, name="out_proj")
        hn = _rmsnorm(h, norm_mlp3, layer, 512)
        u = _matmul(hn, w_mlp_in, layer, 2048, 1024, relu2=True, out_dtype=BF16, name="mlp_in")
        h = _matmul_kgrid(u, w_mlp_out, layer, h, 2048, 1024, 1024, name="mlp_out")
    return h.reshape(B, L, D)
```

```python
import functools
import math

import jax
import jax.numpy as jnp
import numpy as np
from jax import lax
from jax.experimental import pallas as pl
from jax.experimental.pallas import tpu as pltpu

HEAD_DIM = 128
V_HEAD_DIM = 2 * HEAD_DIM
NORM_EPS = 1e-6
GN_EPS = 1e-5
ROPE_THETA = 10000.0
LOG2E = 1.4426950408889634
RET_CHUNK = 256
RET_UNROLL = 32
NEG_BIG = -1e30
MAX_EXP2_ARG = 50.0
SCORE_MARGIN = 1.01
VMEM_LIMIT_V7X = 56 * 1024 * 1024

TILES = {
    "rmsnorm": 512,
    "in_proj": (1024, 1024),
    "attention": (1024, 1024),
    "merge": (1024, 512),
    "out_proj": 512,
    "mlp_in": (2048, 1024),
    "mlp_out": (2048, 1024, 1024),
}

F32 = jnp.float32
BF16 = jnp.bfloat16
_NT = (((1,), (1,)), ((), ()))
_TN = (((0,), (0,)), ((), ()))


def _params(*sem):
    return pltpu.CompilerParams(dimension_semantics=sem, vmem_limit_bytes=VMEM_LIMIT_V7X)


def _rmsnorm_kernel(x_ref, w_ref, o_ref):
    x = x_ref[...]
    ms = jnp.mean(x * x, axis=-1, keepdims=True)
    o_ref[...] = (x * lax.rsqrt(ms + NORM_EPS) * w_ref[...]).astype(o_ref.dtype)


def _rmsnorm(x, w3, layer, tm):
    L, D = x.shape
    tm = min(tm, L)
    return pl.pallas_call(
        _rmsnorm_kernel,
        grid=(L // tm,),
        in_specs=[pl.BlockSpec((tm, D), lambda i: (i, 0)),
                  pl.BlockSpec((None, 1, D), lambda i: (layer, 0, 0))],
        out_specs=pl.BlockSpec((tm, D), lambda i: (i, 0)),
        out_shape=jax.ShapeDtypeStruct((L, D), BF16),
        compiler_params=_params("arbitrary"),
        name="rmsnorm",
    )(x, w3)


def _logistic(x):
    return 0.5 * jnp.tanh(0.5 * x) + 0.5


def _proj_kernel(a_ref, w_ref, cos_ref, sin_ref, qw_ref, kw_ref, o_ref, wbf_ref, *, tn, d_model):
    j = pl.program_id(0)

    @pl.when(pl.program_id(1) == 0)
    def _cast():
        wbf_ref[...] = w_ref[...].astype(BF16)

    def matmul():
        return jnp.dot(a_ref[...], wbf_ref[...], preferred_element_type=F32)

    t = d_model // tn
    half = t // 2
    groups = tn // HEAD_DIM

    tm = a_ref.shape[0]
    rows = pl.ds(pl.multiple_of(pl.program_id(1) * tm, tm), tm)

    def rope(y):
        return y * cos_ref[rows, :] + pltpu.roll(y, HEAD_DIM // 2, 1) * sin_ref[rows, :]

    @pl.when(j < t)
    def _ret_qk():
        acc = matmul()
        s = jnp.where(j >= half, HEAD_DIM ** -0.5, 1.0).astype(F32)
        for c in range(groups):
            sl = slice(c * HEAD_DIM, (c + 1) * HEAD_DIM)
            o_ref[:, sl] = (rope(acc[:, sl]) * s).astype(o_ref.dtype)

    @pl.when(((j >= t) & (j < 2 * t)) | ((j >= 5 * t) & (j < 6 * t)))
    def _plain():
        o_ref[...] = matmul().astype(o_ref.dtype)

    @pl.when((j >= 2 * t) & (j < 3 * t))
    def _silu():
        acc = matmul()
        o_ref[...] = (acc * _logistic(acc)).astype(o_ref.dtype)

    @pl.when((j >= 3 * t) & (j < 5 * t))
    def _diff_qk():
        acc = matmul()
        is_q = (j < 4 * t).astype(F32)
        w = is_q * (qw_ref[...] * (HEAD_DIM ** -0.5 * LOG2E)) + (1.0 - is_q) * kw_ref[...]
        mean_w = jnp.full((2 * HEAD_DIM, HEAD_DIM), 1.0 / HEAD_DIM, BF16)
        for c in range(groups):
            sl = slice(c * HEAD_DIM, (c + 1) * HEAD_DIM)
            y = acc[:, sl]
            sq = y * y
            hi = sq.astype(BF16)
            lo = (sq - hi.astype(F32)).astype(BF16)
            ms = jnp.dot(jnp.concatenate([hi, lo], axis=1), mean_w, preferred_element_type=F32)
            y = y * lax.rsqrt(ms + NORM_EPS) * w
            o_ref[:, sl] = rope(y).astype(o_ref.dtype)

    @pl.when(j >= 6 * t)
    def _sigmoid():
        o_ref[...] = _logistic(matmul()).astype(o_ref.dtype)


def _in_proj(xn, w_in, cos, sin_s, qw3, kw3, layer, tm, tn):
    L, D = xn.shape
    n_cols = w_in.shape[-1]
    tm = min(tm, L)
    tn = min(tn, D // 2)
    return pl.pallas_call(
        functools.partial(_proj_kernel, tn=tn, d_model=D),
        grid=(n_cols // tn, L // tm),
        in_specs=[pl.BlockSpec((tm, D), lambda j, i: (i, 0)),
                  pl.BlockSpec((None, D, tn), lambda j, i: (layer, 0, j)),
                  pl.BlockSpec((L, HEAD_DIM), lambda j, i: (0, 0)),
                  pl.BlockSpec((L, HEAD_DIM), lambda j, i: (0, 0)),
                  pl.BlockSpec((None, 1, HEAD_DIM), lambda j, i: (layer, 0, 0)),
                  pl.BlockSpec((None, 1, HEAD_DIM), lambda j, i: (layer, 0, 0))],
        out_specs=pl.BlockSpec((tm, tn), lambda j, i: (i, j)),
        out_shape=jax.ShapeDtypeStruct((L, n_cols), BF16),
        scratch_shapes=[pltpu.VMEM((D, tn), BF16)],
        compiler_params=_params("arbitrary", "arbitrary"),
        name="in_proj",
    )(xn, w_in, cos, sin_s, qw3, kw3)


def _log_sigmoid(x):
    return jnp.minimum(x, 0.0) - jnp.log1p(jnp.exp(-jnp.abs(x)))


def _retention_kernel(lf_ref, lb_ref, q_ref, k_ref, v_ref, g_ref, gnw_ref, o_ref,
                      state_ref, dmask_ref, wq_ref, wk_ref, *, chunk, n_chunks):
    C, N = chunk, n_chunks
    lgf = _log_sigmoid(lf_ref[...])
    lgb = _log_sigmoid(lb_ref[...])
    lgf1, lgb1 = lgf[:, :1], lgb[:, :1]

    row = lax.broadcasted_iota(jnp.int32, (C, HEAD_DIM), 0).astype(F32)
    wq_ref[0] = jnp.exp((row + 1.0) * lgf)
    wq_ref[1] = jnp.exp((C - row) * lgb)
    wk_ref[0] = jnp.exp((C - 1.0 - row) * lgf)
    wk_ref[1] = jnp.exp(row * lgb)
    r = lax.broadcasted_iota(jnp.int32, (C, C), 0)
    c = lax.broadcasted_iota(jnp.int32, (C, C), 1)
    diff = (r - c).astype(F32)
    dmask_ref[...] = jnp.where(diff > 0, jnp.exp(jnp.maximum(diff, 0.0) * lgf1),
                               jnp.where(diff < 0, jnp.exp(jnp.maximum(-diff, 0.0) * lgb1), 2.0))
    dec_f = jnp.exp(C * lgf1)
    dec_b = jnp.exp(C * lgb1)

    def rows(n):
        return pl.ds(pl.multiple_of(n * C, C), C)

    def states(t, carry):
        sf, sb = carry
        nb = N - 1 - t
        kf = (k_ref[rows(t), :].astype(F32) * wk_ref[0]).astype(BF16)
        kb = (k_ref[rows(nb), :].astype(F32) * wk_ref[1]).astype(BF16)
        kvf = lax.dot_general(kf, v_ref[rows(t), :], _TN, preferred_element_type=F32)
        kvb = lax.dot_general(kb, v_ref[rows(nb), :], _TN, preferred_element_type=F32)
        state_ref[t, 0:HEAD_DIM, :] = sf.astype(BF16)
        state_ref[nb, HEAD_DIM:2 * HEAD_DIM, :] = sb.astype(BF16)
        return sf * dec_f + kvf, sb * dec_b + kvb

    zero = jnp.zeros((HEAD_DIM, V_HEAD_DIM), F32)
    lax.fori_loop(0, N, states, (zero, zero), unroll=min(N, RET_UNROLL))

    def chunk_out(n, carry):
        q = q_ref[rows(n), :]
        s = lax.dot_general(q, k_ref[rows(n), :], _NT, preferred_element_type=F32)
        sd = (s * dmask_ref[...]).astype(BF16)
        qf = q.astype(F32)
        qcat = jnp.concatenate([qf * wq_ref[0], qf * wq_ref[1]], axis=1).astype(BF16)
        o = (jnp.dot(sd, v_ref[rows(n), :], preferred_element_type=F32)
             + jnp.dot(qcat, state_ref[n], preferred_element_type=F32))
        mu = jnp.mean(o, axis=-1, keepdims=True)
        xc = o - mu
        var = jnp.mean(xc * xc, axis=-1, keepdims=True)
        y = xc * lax.rsqrt(var + GN_EPS) * gnw_ref[...]
        o_ref[rows(n), :] = (g_ref[rows(n), :].astype(F32) * y).astype(o_ref.dtype)
        return carry

    lax.fori_loop(0, N, chunk_out, 0, unroll=min(N, RET_UNROLL))


def _retention(proj, lf4, lb4, gnw4, layer, d_model):
    L = proj.shape[0]
    H = d_model // V_HEAD_DIM
    C = min(RET_CHUNK, L)
    N = L // C
    qk_blocks = d_model // 2 // HEAD_DIM
    v_blocks = d_model // V_HEAD_DIM
    return pl.pallas_call(
        functools.partial(_retention_kernel, chunk=C, n_chunks=N),
        grid=(H,),
        in_specs=[pl.BlockSpec((None, None, 1, HEAD_DIM), lambda h: (layer, h, 0, 0)),
                  pl.BlockSpec((None, None, 1, HEAD_DIM), lambda h: (layer, h, 0, 0)),
                  pl.BlockSpec((L, HEAD_DIM), lambda h: (0, h)),
                  pl.BlockSpec((L, HEAD_DIM), lambda h: (0, qk_blocks + h)),
                  pl.BlockSpec((L, V_HEAD_DIM), lambda h: (0, v_blocks + h)),
                  pl.BlockSpec((L, V_HEAD_DIM), lambda h: (0, 2 * v_blocks + h)),
                  pl.BlockSpec((None, None, 1, V_HEAD_DIM), lambda h: (layer, h, 0, 0))],
        out_specs=pl.BlockSpec((L, V_HEAD_DIM), lambda h: (0, h)),
        out_shape=jax.ShapeDtypeStruct((L, d_model), BF16),
        scratch_shapes=[pltpu.VMEM((N, 2 * HEAD_DIM, V_HEAD_DIM), BF16),
                        pltpu.VMEM((C, C), F32),
                        pltpu.VMEM((2, C, HEAD_DIM), F32),
                        pltpu.VMEM((2, C, HEAD_DIM), F32)],
        compiler_params=_params("arbitrary"),
        name="retention",
    )(lf4, lb4, proj, proj, proj, proj, gnw4)


def _attn_kernel(lam_ref, qw_ref, kw_ref, q_ref, k_ref, v_ref, w_ref, o_ref, acc_ref, l_ref,
                 *, tk, n_kv, lambda_init):
    tq = q_ref.shape[0]
    q = q_ref[...]
    qs = (q[:, :HEAD_DIM], q[:, HEAD_DIM:])
    score_bound = (HEAD_DIM * (HEAD_DIM ** -0.5 * LOG2E) * SCORE_MARGIN
                   * jnp.max(jnp.abs(qw_ref[...])) * jnp.max(jnp.abs(kw_ref[...])))

    def scores(t, k):
        return lax.dot_general(qs[t], k[:, t * HEAD_DIM:(t + 1) * HEAD_DIM], _NT, preferred_element_type=F32)

    def finish(l1, l2):
        lp = lam_ref[...]
        lam = (jnp.exp(jnp.sum(lp[0:1] * lp[1:2], axis=-1, keepdims=True))
               - jnp.exp(jnp.sum(lp[2:3] * lp[3:4], axis=-1, keepdims=True)) + lambda_init)
        o = acc_ref[0] * (1.0 / l1) - lam * (acc_ref[1] * (1.0 / l2))
        ms = jnp.mean(o * o, axis=-1, keepdims=True)
        y = o * lax.rsqrt(ms + NORM_EPS) * w_ref[...] * (1.0 - lambda_init)
        o_ref[...] = y.astype(o_ref.dtype)

    @pl.when(score_bound <= MAX_EXP2_ARG)
    def _bounded():
        for j in range(n_kv):
            k = k_ref[j * tk:(j + 1) * tk, :]
            v = v_ref[j * tk:(j + 1) * tk, :]
            for t in range(2):
                p = jnp.exp2(scores(t, k))
                lsum = p[:, :HEAD_DIM]
                for c in range(1, tk // HEAD_DIM):
                    lsum = lsum + p[:, c * HEAD_DIM:(c + 1) * HEAD_DIM]
                pv = jnp.dot(p.astype(BF16), v, preferred_element_type=F32)
                if j == 0:
                    l_ref[t] = lsum
                    acc_ref[t] = pv
                else:
                    l_ref[t] += lsum
                    acc_ref[t] += pv
        finish(jnp.sum(l_ref[0], axis=-1, keepdims=True), jnp.sum(l_ref[1], axis=-1, keepdims=True))

    @pl.when(score_bound > MAX_EXP2_ARG)
    def _online():
        acc_ref[...] = jnp.zeros_like(acc_ref)

        def body(j, carry):
            rows = pl.ds(pl.multiple_of(j * tk, tk), tk)
            k = k_ref[rows, :]
            v = v_ref[rows, :]
            out = []
            for t in range(2):
                m, l = carry[2 * t], carry[2 * t + 1]
                s = scores(t, k)
                m_new = jnp.maximum(m, jnp.max(s, axis=-1, keepdims=True))
                alpha = jnp.exp2(m - m_new)
                p = jnp.exp2(s - m_new)
                l_new = alpha * l + jnp.sum(p, axis=-1, keepdims=True)
                acc_ref[t] = alpha * acc_ref[t] + jnp.dot(p.astype(BF16), v, preferred_element_type=F32)
                out += [m_new, l_new]
            return tuple(out)

        m0 = jnp.full((tq, 1), NEG_BIG, F32)
        l0 = jnp.zeros((tq, 1), F32)
        _, l1, _, l2 = lax.fori_loop(0, n_kv, body, (m0, l0, m0, l0))
        finish(l1, l2)


def _diff_attention(proj, diff_lambda, qw3, kw3, subln3, layer, d_model, tq, tk):
    L = proj.shape[0]
    H = d_model // V_HEAD_DIM
    tq = min(tq, L)
    tk = min(tk, L)
    vb = d_model // V_HEAD_DIM
    lambda_init = 0.8 - 0.6 * math.exp(-0.3 * layer)
    return pl.pallas_call(
        functools.partial(_attn_kernel, tk=tk, n_kv=L // tk, lambda_init=lambda_init),
        grid=(H, L // tq),
        in_specs=[pl.BlockSpec((None, 4, HEAD_DIM), lambda h, i: (layer, 0, 0)),
                  pl.BlockSpec((None, 1, HEAD_DIM), lambda h, i: (layer, 0, 0)),
                  pl.BlockSpec((None, 1, HEAD_DIM), lambda h, i: (layer, 0, 0)),
                  pl.BlockSpec((tq, V_HEAD_DIM), lambda h, i: (i, 3 * vb + h)),
                  pl.BlockSpec((L, V_HEAD_DIM), lambda h, i: (0, 4 * vb + h)),
                  pl.BlockSpec((L, V_HEAD_DIM), lambda h, i: (0, 5 * vb + h)),
                  pl.BlockSpec((None, 1, V_HEAD_DIM), lambda h, i: (layer, 0, 0))],
        out_specs=pl.BlockSpec((tq, V_HEAD_DIM), lambda h, i: (i, h)),
        out_shape=jax.ShapeDtypeStruct((L, d_model), BF16),
        scratch_shapes=[pltpu.VMEM((2, tq, V_HEAD_DIM), F32),
                        pltpu.VMEM((2, tq, HEAD_DIM), F32)],
        compiler_params=_params("arbitrary", "arbitrary"),
        name="diff_attention",
    )(diff_lambda, qw3, kw3, proj, proj, proj, subln3)


def _merge_kernel(a1_ref, a2_ref, w1_ref, w2_ref, g1_ref, g2_ref, o_ref, w1bf_ref, w2bf_ref):
    @pl.when(pl.program_id(1) == 0)
    def _cast():
        w1bf_ref[...] = w1_ref[...].astype(BF16)
        w2bf_ref[...] = w2_ref[...].astype(BF16)

    y1 = jnp.dot(a1_ref[...], w1bf_ref[...], preferred_element_type=F32)
    y2 = jnp.dot(a2_ref[...], w2bf_ref[...], preferred_element_type=F32)
    o_ref[...] = (g1_ref[...].astype(F32) * y1 + g2_ref[...].astype(F32) * y2).astype(o_ref.dtype)


def _merge(ret, att, w_ret_out, w_diff_out, proj, layer, tm, tn):
    L, D = ret.shape
    tm = min(tm, L)
    tn = min(tn, D)
    gb = D // tn
    return pl.pallas_call(
        _merge_kernel,
        grid=(D // tn, L // tm),
        in_specs=[pl.BlockSpec((tm, D), lambda j, i: (i, 0)),
                  pl.BlockSpec((tm, D), lambda j, i: (i, 0)),
                  pl.BlockSpec((None, D, tn), lambda j, i: (layer, 0, j)),
                  pl.BlockSpec((None, D, tn), lambda j, i: (layer, 0, j)),
                  pl.BlockSpec((tm, tn), lambda j, i: (i, 6 * gb + j)),
                  pl.BlockSpec((tm, tn), lambda j, i: (i, 7 * gb + j))],
        out_specs=pl.BlockSpec((tm, tn), lambda j, i: (i, j)),
        out_shape=jax.ShapeDtypeStruct((L, D), BF16),
        scratch_shapes=[pltpu.VMEM((D, tn), BF16), pltpu.VMEM((D, tn), BF16)],
        compiler_params=_params("arbitrary", "arbitrary"),
        name="merge",
    )(ret, att, w_ret_out, w_diff_out, proj, proj)


def _out_proj_norm_kernel(a_ref, w_ref, x_ref, nw_ref, o_ref, hn_ref, wbf_ref):
    @pl.when(pl.program_id(0) == 0)
    def _cast():
        wbf_ref[...] = w_ref[...].astype(BF16)

    y = x_ref[...] + jnp.dot(a_ref[...], wbf_ref[...], preferred_element_type=F32)
    o_ref[...] = y
    ms = jnp.mean(y * y, axis=-1, keepdims=True)
    hn_ref[...] = (y * lax.rsqrt(ms + NORM_EPS) * nw_ref[...]).astype(hn_ref.dtype)


def _out_proj_norm(a, w, x, nw3, layer, tm):
    L, D = x.shape
    tm = min(tm, L)
    return pl.pallas_call(
        _out_proj_norm_kernel,
        grid=(L // tm,),
        in_specs=[pl.BlockSpec((tm, D), lambda i: (i, 0)),
                  pl.BlockSpec((None, D, D), lambda i: (layer, 0, 0)),
                  pl.BlockSpec((tm, D), lambda i: (i, 0)),
                  pl.BlockSpec((None, 1, D), lambda i: (layer, 0, 0))],
        out_specs=[pl.BlockSpec((tm, D), lambda i: (i, 0)),
                   pl.BlockSpec((tm, D), lambda i: (i, 0))],
        out_shape=[jax.ShapeDtypeStruct((L, D), F32), jax.ShapeDtypeStruct((L, D), BF16)],
        scratch_shapes=[pltpu.VMEM((D, D), BF16)],
        compiler_params=_params("arbitrary"),
        name="out_proj",
    )(a, w, x, nw3)


def _mlp_in_kernel(a_ref, w_ref, o_ref, wbf_ref):
    @pl.when(pl.program_id(1) == 0)
    def _cast():
        wbf_ref[...] = w_ref[...].astype(BF16)

    y = jnp.dot(a_ref[...], wbf_ref[...], preferred_element_type=F32)
    o_ref[...] = jnp.square(jnp.maximum(y, 0.0)).astype(o_ref.dtype)


def _mlp_in(a, w, layer, tm, tn):
    L, K = a.shape
    n_cols = w.shape[-1]
    tm = min(tm, L)
    tn = min(tn, n_cols)
    return pl.pallas_call(
        _mlp_in_kernel,
        grid=(n_cols // tn, L // tm),
        in_specs=[pl.BlockSpec((tm, K), lambda j, i: (i, 0)),
                  pl.BlockSpec((None, K, tn), lambda j, i: (layer, 0, j))],
        out_specs=pl.BlockSpec((tm, tn), lambda j, i: (i, j)),
        out_shape=jax.ShapeDtypeStruct((L, n_cols), BF16),
        scratch_shapes=[pltpu.VMEM((K, tn), BF16)],
        compiler_params=_params("arbitrary", "arbitrary"),
        name="mlp_in",
    )(a, w)


def _matmul_kgrid_kernel(a_ref, w_ref, x_ref, o_ref):
    @pl.when(pl.program_id(2) == 0)
    def _init():
        o_ref[...] = x_ref[...]

    o_ref[...] += jnp.dot(a_ref[...], w_ref[...].astype(BF16), preferred_element_type=F32)


def _matmul_kgrid(a, w, layer, x, tm, tn, tk, *, name):
    L, K = a.shape
    n_cols = w.shape[-1]
    tm, tn, tk = min(tm, L), min(tn, n_cols), min(tk, K)
    return pl.pallas_call(
        _matmul_kgrid_kernel,
        grid=(n_cols // tn, L // tm, K // tk),
        in_specs=[pl.BlockSpec((tm, tk), lambda j, i, k: (i, k)),
                  pl.BlockSpec((None, tk, tn), lambda j, i, k: (layer, k, j)),
                  pl.BlockSpec((tm, tn), lambda j, i, k: (i, j))],
        out_specs=pl.BlockSpec((tm, tn), lambda j, i, k: (i, j)),
        out_shape=jax.ShapeDtypeStruct((L, n_cols), F32),
        compiler_params=_params("arbitrary", "arbitrary", "arbitrary"),
        name=name,
    )(a, w, x)


def _rope_tables(L):
    inv = 1.0 / (np.float32(ROPE_THETA) ** (np.arange(0, HEAD_DIM, 2, dtype=np.float32) / np.float32(HEAD_DIM)))
    ang = np.arange(L, dtype=np.float32)[:, None] * inv.astype(np.float32)[None, :]
    ang = np.concatenate([ang, ang], axis=-1)
    sign = np.where(np.arange(HEAD_DIM) < HEAD_DIM // 2, -1.0, 1.0).astype(np.float32)
    return jnp.asarray(np.cos(ang), F32), jnp.asarray(np.sin(ang) * sign[None, :], F32)


def kernel(x, norm_mix_w, w_in, ret_decay_fwd, ret_decay_bwd, ret_gn_w, w_ret_out, q_norm_w, k_norm_w,
           diff_lambda, diff_subln_w, w_diff_out, w_out, norm_mlp_w, w_mlp_in, w_mlp_out):
    B, L, D = x.shape
    assert B == 1 and D % V_HEAD_DIM == 0
    depth = w_in.shape[0]
    H = D // V_HEAD_DIM
    cos, sin_s = _rope_tables(L)

    norm_mix3 = norm_mix_w.reshape(depth, 1, D)
    norm_mlp3 = norm_mlp_w.reshape(depth, 1, D)
    qw3 = q_norm_w.reshape(depth, 1, HEAD_DIM)
    kw3 = k_norm_w.reshape(depth, 1, HEAD_DIM)
    subln3 = diff_subln_w.reshape(depth, 1, V_HEAD_DIM)
    gnw4 = ret_gn_w.reshape(depth, H, 1, V_HEAD_DIM)
    lf4 = jnp.broadcast_to(ret_decay_fwd[:, :, None, None], (depth, H, 1, HEAD_DIM))
    lb4 = jnp.broadcast_to(ret_decay_bwd[:, :, None, None], (depth, H, 1, HEAD_DIM))

    h = x.reshape(L, D)
    for layer in range(depth):
        xn = _rmsnorm(h, norm_mix3, layer, TILES["rmsnorm"])
        proj = _in_proj(xn, w_in, cos, sin_s, qw3, kw3, layer, *TILES["in_proj"])
        ret = _retention(proj, lf4, lb4, gnw4, layer, D)
        att = _diff_attention(proj, diff_lambda, qw3, kw3, subln3, layer, D, *TILES["attention"])
        merged = _merge(ret, att, w_ret_out, w_diff_out, proj, layer, *TILES["merge"])
        h, hn = _out_proj_norm(merged, w_out, h, norm_mlp3, layer, TILES["out_proj"])
        u = _mlp_in(hn, w_mlp_in, layer, *TILES["mlp_in"])
        h = _matmul_kgrid(u, w_mlp_out, layer, h, *TILES["mlp_out"], name="mlp_out")
    return h.reshape(B, L, D)
```

```python
import functools
import math

import jax
import jax.numpy as jnp
import numpy as np
from jax import lax
from jax.experimental import pallas as pl
from jax.experimental.pallas import tpu as pltpu

HEAD_DIM = 128
V_HEAD_DIM = 2 * HEAD_DIM
NORM_EPS = 1e-6
GN_EPS = 1e-5
ROPE_THETA = 10000.0
LOG2E = 1.4426950408889634
RET_CHUNK = 256
RET_UNROLL = 32
NEG_BIG = -1e30
MAX_EXP2_ARG = 50.0
SCORE_MARGIN = 1.01
VMEM_LIMIT_V7X = 56 * 1024 * 1024

TILES = {
    "rmsnorm": 512,
    "in_proj": (1024, 1024),
    "attention": (1024, 1024),
    "merge": (1024, 1024),
    "out_proj": 512,
    "mlp_in": (2048, 1024),
    "mlp_out": (1024, 1024),
}

F32 = jnp.float32
BF16 = jnp.bfloat16
_NT = (((1,), (1,)), ((), ()))
_TN = (((0,), (0,)), ((), ()))


def _params(*sem):
    return pltpu.CompilerParams(dimension_semantics=sem, vmem_limit_bytes=VMEM_LIMIT_V7X)


def _rmsnorm_kernel(x_ref, w_ref, o_ref):
    x = x_ref[...]
    ms = jnp.mean(x * x, axis=-1, keepdims=True)
    o_ref[...] = (x * lax.rsqrt(ms + NORM_EPS) * w_ref[...]).astype(o_ref.dtype)


def _rmsnorm(x, w3, layer, tm):
    L, D = x.shape
    tm = min(tm, L)
    return pl.pallas_call(
        _rmsnorm_kernel,
        grid=(L // tm,),
        in_specs=[pl.BlockSpec((tm, D), lambda i: (i, 0)),
                  pl.BlockSpec((None, 1, D), lambda i: (layer, 0, 0))],
        out_specs=pl.BlockSpec((tm, D), lambda i: (i, 0)),
        out_shape=jax.ShapeDtypeStruct((L, D), BF16),
        compiler_params=_params("arbitrary"),
        name="rmsnorm",
    )(x, w3)


def _logistic(x):
    return 0.5 * jnp.tanh(0.5 * x) + 0.5


def _proj_kernel(a_ref, w_ref, cos_ref, sin_ref, qw_ref, kw_ref, o_ref, wbf_ref, *, tn, d_model):
    j = pl.program_id(0)

    @pl.when(pl.program_id(1) == 0)
    def _cast():
        wbf_ref[...] = w_ref[...].astype(BF16)

    def matmul():
        return jnp.dot(a_ref[...], wbf_ref[...], preferred_element_type=F32)

    t = d_model // tn
    half = t // 2
    groups = tn // HEAD_DIM

    tm = a_ref.shape[0]
    rows = pl.ds(pl.multiple_of(pl.program_id(1) * tm, tm), tm)

    def rope(y):
        return y * cos_ref[rows, :] + pltpu.roll(y, HEAD_DIM // 2, 1) * sin_ref[rows, :]

    @pl.when(j < t)
    def _ret_qk():
        acc = matmul()
        s = jnp.where(j >= half, HEAD_DIM ** -0.5, 1.0).astype(F32)
        for c in range(groups):
            sl = slice(c * HEAD_DIM, (c + 1) * HEAD_DIM)
            o_ref[:, sl] = (rope(acc[:, sl]) * s).astype(o_ref.dtype)

    @pl.when(((j >= t) & (j < 2 * t)) | ((j >= 5 * t) & (j < 6 * t)))
    def _plain():
        o_ref[...] = matmul().astype(o_ref.dtype)

    @pl.when((j >= 2 * t) & (j < 3 * t))
    def _silu():
        acc = matmul()
        o_ref[...] = (acc * _logistic(acc)).astype(o_ref.dtype)

    @pl.when((j >= 3 * t) & (j < 5 * t))
    def _diff_qk():
        acc = matmul()
        is_q = (j < 4 * t).astype(F32)
        w = is_q * (qw_ref[...] * (HEAD_DIM ** -0.5 * LOG2E)) + (1.0 - is_q) * kw_ref[...]
        mean_w = jnp.full((2 * HEAD_DIM, HEAD_DIM), 1.0 / HEAD_DIM, BF16)
        for c in range(groups):
            sl = slice(c * HEAD_DIM, (c + 1) * HEAD_DIM)
            y = acc[:, sl]
            sq = y * y
            hi = sq.astype(BF16)
            lo = (sq - hi.astype(F32)).astype(BF16)
            ms = jnp.dot(jnp.concatenate([hi, lo], axis=1), mean_w, preferred_element_type=F32)
            y = y * lax.rsqrt(ms + NORM_EPS) * w
            o_ref[:, sl] = rope(y).astype(o_ref.dtype)

    @pl.when(j >= 6 * t)
    def _sigmoid():
        o_ref[...] = _logistic(matmul()).astype(o_ref.dtype)


def _in_proj(xn, w_in, cos, sin_s, qw3, kw3, layer, tm, tn):
    L, D = xn.shape
    n_cols = w_in.shape[-1]
    tm = min(tm, L)
    tn = min(tn, D // 2)
    return pl.pallas_call(
        functools.partial(_proj_kernel, tn=tn, d_model=D),
        grid=(n_cols // tn, L // tm),
        in_specs=[pl.BlockSpec((tm, D), lambda j, i: (i, 0)),
                  pl.BlockSpec((None, D, tn), lambda j, i: (layer, 0, j)),
                  pl.BlockSpec((L, HEAD_DIM), lambda j, i: (0, 0)),
                  pl.BlockSpec((L, HEAD_DIM), lambda j, i: (0, 0)),
                  pl.BlockSpec((None, 1, HEAD_DIM), lambda j, i: (layer, 0, 0)),
                  pl.BlockSpec((None, 1, HEAD_DIM), lambda j, i: (layer, 0, 0))],
        out_specs=pl.BlockSpec((tm, tn), lambda j, i: (i, j)),
        out_shape=jax.ShapeDtypeStruct((L, n_cols), BF16),
        scratch_shapes=[pltpu.VMEM((D, tn), BF16)],
        compiler_params=_params("arbitrary", "arbitrary"),
        name="in_proj",
    )(xn, w_in, cos, sin_s, qw3, kw3)


def _log_sigmoid(x):
    return jnp.minimum(x, 0.0) - jnp.log1p(jnp.exp(-jnp.abs(x)))


def _retention_kernel(lf_ref, lb_ref, q_ref, k_ref, v_ref, g_ref, gnw_ref, o_ref,
                      state_ref, dmask_ref, wq_ref, wk_ref, *, chunk, n_chunks):
    C, N = chunk, n_chunks
    lgf = _log_sigmoid(lf_ref[...])
    lgb = _log_sigmoid(lb_ref[...])
    lgf1, lgb1 = lgf[:, :1], lgb[:, :1]

    row = lax.broadcasted_iota(jnp.int32, (C, HEAD_DIM), 0).astype(F32)
    wq_ref[0] = jnp.exp((row + 1.0) * lgf)
    wq_ref[1] = jnp.exp((C - row) * lgb)
    wk_ref[0] = jnp.exp((C - 1.0 - row) * lgf)
    wk_ref[1] = jnp.exp(row * lgb)
    r = lax.broadcasted_iota(jnp.int32, (C, C), 0)
    c = lax.broadcasted_iota(jnp.int32, (C, C), 1)
    diff = (r - c).astype(F32)
    dmask_ref[...] = jnp.where(diff > 0, jnp.exp(jnp.maximum(diff, 0.0) * lgf1),
                               jnp.where(diff < 0, jnp.exp(jnp.maximum(-diff, 0.0) * lgb1), 2.0))
    dec_f = jnp.exp(C * lgf1)
    dec_b = jnp.exp(C * lgb1)

    def rows(n):
        return pl.ds(pl.multiple_of(n * C, C), C)

    def states(t, carry):
        sf, sb = carry
        nb = N - 1 - t
        kf = (k_ref[rows(t), :].astype(F32) * wk_ref[0]).astype(BF16)
        kb = (k_ref[rows(nb), :].astype(F32) * wk_ref[1]).astype(BF16)
        kvf = lax.dot_general(kf, v_ref[rows(t), :], _TN, preferred_element_type=F32)
        kvb = lax.dot_general(kb, v_ref[rows(nb), :], _TN, preferred_element_type=F32)
        state_ref[t, 0:HEAD_DIM, :] = sf.astype(BF16)
        state_ref[nb, HEAD_DIM:2 * HEAD_DIM, :] = sb.astype(BF16)
        return sf * dec_f + kvf, sb * dec_b + kvb

    zero = jnp.zeros((HEAD_DIM, V_HEAD_DIM), F32)
    lax.fori_loop(0, N, states, (zero, zero), unroll=min(N, RET_UNROLL))

    def chunk_out(n, carry):
        q = q_ref[rows(n), :]
        s = lax.dot_general(q, k_ref[rows(n), :], _NT, preferred_element_type=F32)
        sd = (s * dmask_ref[...]).astype(BF16)
        qf = q.astype(F32)
        qcat = jnp.concatenate([qf * wq_ref[0], qf * wq_ref[1]], axis=1).astype(BF16)
        o = (jnp.dot(sd, v_ref[rows(n), :], preferred_element_type=F32)
             + jnp.dot(qcat, state_ref[n], preferred_element_type=F32))
        mu = jnp.mean(o, axis=-1, keepdims=True)
        xc = o - mu
        var = jnp.mean(xc * xc, axis=-1, keepdims=True)
        y = xc * lax.rsqrt(var + GN_EPS) * gnw_ref[...]
        o_ref[rows(n), :] = (g_ref[rows(n), :].astype(F32) * y).astype(o_ref.dtype)
        return carry

    lax.fori_loop(0, N, chunk_out, 0, unroll=min(N, RET_UNROLL))


def _retention(proj, lf4, lb4, gnw4, layer, d_model):
    L = proj.shape[0]
    H = d_model // V_HEAD_DIM
    C = min(RET_CHUNK, L)
    N = L // C
    qk_blocks = d_model // 2 // HEAD_DIM
    v_blocks = d_model // V_HEAD_DIM
    return pl.pallas_call(
        functools.partial(_retention_kernel, chunk=C, n_chunks=N),
        grid=(H,),
        in_specs=[pl.BlockSpec((None, None, 1, HEAD_DIM), lambda h: (layer, h, 0, 0)),
                  pl.BlockSpec((None, None, 1, HEAD_DIM), lambda h: (layer, h, 0, 0)),
                  pl.BlockSpec((L, HEAD_DIM), lambda h: (0, h)),
                  pl.BlockSpec((L, HEAD_DIM), lambda h: (0, qk_blocks + h)),
                  pl.BlockSpec((L, V_HEAD_DIM), lambda h: (0, v_blocks + h)),
                  pl.BlockSpec((L, V_HEAD_DIM), lambda h: (0, 2 * v_blocks + h)),
                  pl.BlockSpec((None, None, 1, V_HEAD_DIM), lambda h: (layer, h, 0, 0))],
        out_specs=pl.BlockSpec((L, V_HEAD_DIM), lambda h: (0, h)),
        out_shape=jax.ShapeDtypeStruct((L, d_model), BF16),
        scratch_shapes=[pltpu.VMEM((N, 2 * HEAD_DIM, V_HEAD_DIM), BF16),
                        pltpu.VMEM((C, C), F32),
                        pltpu.VMEM((2, C, HEAD_DIM), F32),
                        pltpu.VMEM((2, C, HEAD_DIM), F32)],
        compiler_params=_params("arbitrary"),
        name="retention",
    )(lf4, lb4, proj, proj, proj, proj, gnw4)


def _attn_kernel(lam_ref, qw_ref, kw_ref, q_ref, k_ref, v_ref, w_ref, *rest, tk, n_kv, lambda_init, n_cast):
    cast_in, o_ref, cast_out = rest[:n_cast], rest[n_cast], rest[n_cast + 1:2 * n_cast + 1]
    acc_ref, l_ref = rest[2 * n_cast + 1:]
    tq = q_ref.shape[0]
    q = q_ref[...]
    qs = (q[:, :HEAD_DIM], q[:, HEAD_DIM:])

    def cast_weight_slabs():
        for src, dst in zip(cast_in, cast_out):
            dst[...] = src[...].astype(BF16)

    score_bound = (HEAD_DIM * (HEAD_DIM ** -0.5 * LOG2E) * SCORE_MARGIN
                   * jnp.max(jnp.abs(qw_ref[...])) * jnp.max(jnp.abs(kw_ref[...])))

    def scores(t, k):
        return lax.dot_general(qs[t], k[:, t * HEAD_DIM:(t + 1) * HEAD_DIM], _NT, preferred_element_type=F32)

    def finish(l1, l2):
        lp = lam_ref[...]
        lam = (jnp.exp(jnp.sum(lp[0:1] * lp[1:2], axis=-1, keepdims=True))
               - jnp.exp(jnp.sum(lp[2:3] * lp[3:4], axis=-1, keepdims=True)) + lambda_init)
        o = acc_ref[0] * (1.0 / l1) - lam * (acc_ref[1] * (1.0 / l2))
        ms = jnp.mean(o * o, axis=-1, keepdims=True)
        y = o * lax.rsqrt(ms + NORM_EPS) * w_ref[...] * (1.0 - lambda_init)
        o_ref[...] = y.astype(o_ref.dtype)

    @pl.when(score_bound <= MAX_EXP2_ARG)
    def _bounded():
        cast_weight_slabs()
        for j in range(n_kv):
            k = k_ref[j * tk:(j + 1) * tk, :]
            v = v_ref[j * tk:(j + 1) * tk, :]
            for t in range(2):
                p = jnp.exp2(scores(t, k))
                lsum = p[:, :HEAD_DIM]
                for c in range(1, tk // HEAD_DIM):
                    lsum = lsum + p[:, c * HEAD_DIM:(c + 1) * HEAD_DIM]
                pv = jnp.dot(p.astype(BF16), v, preferred_element_type=F32)
                if j == 0:
                    l_ref[t] = lsum
                    acc_ref[t] = pv
                else:
                    l_ref[t] += lsum
                    acc_ref[t] += pv
        finish(jnp.sum(l_ref[0], axis=-1, keepdims=True), jnp.sum(l_ref[1], axis=-1, keepdims=True))

    @pl.when(score_bound > MAX_EXP2_ARG)
    def _online():
        cast_weight_slabs()
        acc_ref[...] = jnp.zeros_like(acc_ref)

        def body(j, carry):
            rows = pl.ds(pl.multiple_of(j * tk, tk), tk)
            k = k_ref[rows, :]
            v = v_ref[rows, :]
            out = []
            for t in range(2):
                m, l = carry[2 * t], carry[2 * t + 1]
                s = scores(t, k)
                m_new = jnp.maximum(m, jnp.max(s, axis=-1, keepdims=True))
                alpha = jnp.exp2(m - m_new)
                p = jnp.exp2(s - m_new)
                l_new = alpha * l + jnp.sum(p, axis=-1, keepdims=True)
                acc_ref[t] = alpha * acc_ref[t] + jnp.dot(p.astype(BF16), v, preferred_element_type=F32)
                out += [m_new, l_new]
            return tuple(out)

        m0 = jnp.full((tq, 1), NEG_BIG, F32)
        l0 = jnp.zeros((tq, 1), F32)
        _, l1, _, l2 = lax.fori_loop(0, n_kv, body, (m0, l0, m0, l0))
        finish(l1, l2)


def _diff_attention(proj, diff_lambda, qw3, kw3, subln3, weights, layer, d_model, tq, tk):
    L = proj.shape[0]
    H = d_model // V_HEAD_DIM
    tq = min(tq, L)
    tk = min(tk, L)
    vb = d_model // V_HEAD_DIM
    lambda_init = 0.8 - 0.6 * math.exp(-0.3 * layer)
    nq = L // tq
    steps = H * nq
    cast_in, cast_out, cast_shapes = [], [], []
    for w in weights:
        rows, cols = w.shape[1:]
        slab = rows // steps
        cast_in.append(pl.BlockSpec((None, slab, cols), lambda h, i: (layer, h * nq + i, 0)))
        cast_out.append(pl.BlockSpec((slab, cols), lambda h, i: (h * nq + i, 0)))
        cast_shapes.append(jax.ShapeDtypeStruct((rows, cols), BF16))
    return pl.pallas_call(
        functools.partial(_attn_kernel, tk=tk, n_kv=L // tk, lambda_init=lambda_init, n_cast=len(weights)),
        grid=(H, L // tq),
        in_specs=[pl.BlockSpec((None, 4, HEAD_DIM), lambda h, i: (layer, 0, 0)),
                  pl.BlockSpec((None, 1, HEAD_DIM), lambda h, i: (layer, 0, 0)),
                  pl.BlockSpec((None, 1, HEAD_DIM), lambda h, i: (layer, 0, 0)),
                  pl.BlockSpec((tq, V_HEAD_DIM), lambda h, i: (i, 3 * vb + h)),
                  pl.BlockSpec((L, V_HEAD_DIM), lambda h, i: (0, 4 * vb + h)),
                  pl.BlockSpec((L, V_HEAD_DIM), lambda h, i: (0, 5 * vb + h)),
                  pl.BlockSpec((None, 1, V_HEAD_DIM), lambda h, i: (layer, 0, 0))] + cast_in,
        out_specs=[pl.BlockSpec((tq, V_HEAD_DIM), lambda h, i: (i, h))] + cast_out,
        out_shape=[jax.ShapeDtypeStruct((L, d_model), BF16)] + cast_shapes,
        scratch_shapes=[pltpu.VMEM((2, tq, V_HEAD_DIM), F32),
                        pltpu.VMEM((2, tq, HEAD_DIM), F32)],
        compiler_params=_params("arbitrary", "arbitrary"),
        name="diff_attention",
    )(diff_lambda, qw3, kw3, proj, proj, proj, subln3, *weights)


def _merge_kernel(a1_ref, a2_ref, w1_ref, w2_ref, g1_ref, g2_ref, o_ref):
    y1 = jnp.dot(a1_ref[...], w1_ref[...], preferred_element_type=F32)
    y2 = jnp.dot(a2_ref[...], w2_ref[...], preferred_element_type=F32)
    o_ref[...] = (g1_ref[...].astype(F32) * y1 + g2_ref[...].astype(F32) * y2).astype(o_ref.dtype)


def _merge(ret, att, w_ret_out, w_diff_out, proj, tm, tn):
    L, D = ret.shape
    tm = min(tm, L)
    tn = min(tn, D)
    gb = D // tn
    return pl.pallas_call(
        _merge_kernel,
        grid=(D // tn, L // tm),
        in_specs=[pl.BlockSpec((tm, D), lambda j, i: (i, 0)),
                  pl.BlockSpec((tm, D), lambda j, i: (i, 0)),
                  pl.BlockSpec((D, tn), lambda j, i: (0, j)),
                  pl.BlockSpec((D, tn), lambda j, i: (0, j)),
                  pl.BlockSpec((tm, tn), lambda j, i: (i, 6 * gb + j)),
                  pl.BlockSpec((tm, tn), lambda j, i: (i, 7 * gb + j))],
        out_specs=pl.BlockSpec((tm, tn), lambda j, i: (i, j)),
        out_shape=jax.ShapeDtypeStruct((L, D), BF16),
        compiler_params=_params("arbitrary", "arbitrary"),
        name="merge",
    )(ret, att, w_ret_out, w_diff_out, proj, proj)


def _out_proj_norm_kernel(a_ref, w_ref, x_ref, nw_ref, o_ref, hn_ref):
    y = x_ref[...] + jnp.dot(a_ref[...], w_ref[...], preferred_element_type=F32)
    o_ref[...] = y
    ms = jnp.mean(y * y, axis=-1, keepdims=True)
    hn_ref[...] = (y * lax.rsqrt(ms + NORM_EPS) * nw_ref[...]).astype(hn_ref.dtype)


def _out_proj_norm(a, w, x, nw3, layer, tm):
    L, D = x.shape
    tm = min(tm, L)
    return pl.pallas_call(
        _out_proj_norm_kernel,
        grid=(L // tm,),
        in_specs=[pl.BlockSpec((tm, D), lambda i: (i, 0)),
                  pl.BlockSpec((D, D), lambda i: (0, 0)),
                  pl.BlockSpec((tm, D), lambda i: (i, 0)),
                  pl.BlockSpec((None, 1, D), lambda i: (layer, 0, 0))],
        out_specs=[pl.BlockSpec((tm, D), lambda i: (i, 0)),
                   pl.BlockSpec((tm, D), lambda i: (i, 0))],
        out_shape=[jax.ShapeDtypeStruct((L, D), F32), jax.ShapeDtypeStruct((L, D), BF16)],
        compiler_params=_params("arbitrary"),
        name="out_proj",
    )(a, w, x, nw3)


def _mlp_in_kernel(a_ref, w_ref, o_ref):
    y = jnp.dot(a_ref[...], w_ref[...], preferred_element_type=F32)
    o_ref[...] = jnp.square(jnp.maximum(y, 0.0)).astype(o_ref.dtype)


def _mlp_in(a, w, tm, tn):
    L, K = a.shape
    n_cols = w.shape[-1]
    tm = min(tm, L)
    tn = min(tn, n_cols)
    return pl.pallas_call(
        _mlp_in_kernel,
        grid=(n_cols // tn, L // tm),
        in_specs=[pl.BlockSpec((tm, K), lambda j, i: (i, 0)),
                  pl.BlockSpec((K, tn), lambda j, i: (0, j))],
        out_specs=pl.BlockSpec((tm, tn), lambda j, i: (i, j)),
        out_shape=jax.ShapeDtypeStruct((L, n_cols), BF16),
        compiler_params=_params("arbitrary", "arbitrary"),
        name="mlp_in",
    )(a, w)


def _mlp_out_kernel(a_ref, w_ref, x_ref, *rest, emit_norm):
    if emit_norm:
        nw_ref, o_ref, xn_ref = rest
    else:
        (o_ref,) = rest
    k = pl.program_id(1)

    @pl.when(k == 0)
    def _init():
        o_ref[...] = x_ref[...]

    o_ref[...] += jnp.dot(a_ref[...], w_ref[...], preferred_element_type=F32)

    if emit_norm:
        @pl.when(k == pl.num_programs(1) - 1)
        def _norm():
            y = o_ref[...]
            ms = jnp.mean(y * y, axis=-1, keepdims=True)
            xn_ref[...] = (y * lax.rsqrt(ms + NORM_EPS) * nw_ref[...]).astype(xn_ref.dtype)


def _mlp_out(a, w, x, nw3, next_layer, tm, tk):
    L, K = a.shape
    D = w.shape[-1]
    tm, tk = min(tm, L), min(tk, K)
    emit_norm = next_layer is not None
    in_specs = [pl.BlockSpec((tm, tk), lambda i, k: (i, k)),
                pl.BlockSpec((tk, D), lambda i, k: (k, 0)),
                pl.BlockSpec((tm, D), lambda i, k: (i, 0))]
    out_specs = [pl.BlockSpec((tm, D), lambda i, k: (i, 0))]
    out_shape = [jax.ShapeDtypeStruct((L, D), F32)]
    args = [a, w, x]
    if emit_norm:
        in_specs.append(pl.BlockSpec((None, 1, D), lambda i, k: (next_layer, 0, 0)))
        out_specs.append(pl.BlockSpec((tm, D), lambda i, k: (i, 0)))
        out_shape.append(jax.ShapeDtypeStruct((L, D), BF16))
        args.append(nw3)
    out = pl.pallas_call(
        functools.partial(_mlp_out_kernel, emit_norm=emit_norm),
        grid=(L // tm, K // tk),
        in_specs=in_specs,
        out_specs=out_specs,
        out_shape=out_shape,
        compiler_params=_params("arbitrary", "arbitrary"),
        name="mlp_out",
    )(*args)
    return (out[0], out[1]) if emit_norm else (out[0], None)


def _rope_tables(L):
    inv = 1.0 / (np.float32(ROPE_THETA) ** (np.arange(0, HEAD_DIM, 2, dtype=np.float32) / np.float32(HEAD_DIM)))
    ang = np.arange(L, dtype=np.float32)[:, None] * inv.astype(np.float32)[None, :]
    ang = np.concatenate([ang, ang], axis=-1)
    sign = np.where(np.arange(HEAD_DIM) < HEAD_DIM // 2, -1.0, 1.0).astype(np.float32)
    return jnp.asarray(np.cos(ang), F32), jnp.asarray(np.sin(ang) * sign[None, :], F32)


def kernel(x, norm_mix_w, w_in, ret_decay_fwd, ret_decay_bwd, ret_gn_w, w_ret_out, q_norm_w, k_norm_w,
           diff_lambda, diff_subln_w, w_diff_out, w_out, norm_mlp_w, w_mlp_in, w_mlp_out):
    B, L, D = x.shape
    assert B == 1 and D % V_HEAD_DIM == 0
    depth = w_in.shape[0]
    H = D // V_HEAD_DIM
    cos, sin_s = _rope_tables(L)

    norm_mix3 = norm_mix_w.reshape(depth, 1, D)
    norm_mlp3 = norm_mlp_w.reshape(depth, 1, D)
    qw3 = q_norm_w.reshape(depth, 1, HEAD_DIM)
    kw3 = k_norm_w.reshape(depth, 1, HEAD_DIM)
    subln3 = diff_subln_w.reshape(depth, 1, V_HEAD_DIM)
    gnw4 = ret_gn_w.reshape(depth, H, 1, V_HEAD_DIM)
    lf4 = jnp.broadcast_to(ret_decay_fwd[:, :, None, None], (depth, H, 1, HEAD_DIM))
    lb4 = jnp.broadcast_to(ret_decay_bwd[:, :, None, None], (depth, H, 1, HEAD_DIM))

    h = x.reshape(L, D)
    xn = _rmsnorm(h, norm_mix3, 0, TILES["rmsnorm"])
    for layer in range(depth):
        proj = _in_proj(xn, w_in, cos, sin_s, qw3, kw3, layer, *TILES["in_proj"])
        ret = _retention(proj, lf4, lb4, gnw4, layer, D)
        att, w_ret_bf, w_diff_bf, w_out_bf, w_mlp_in_bf, w_mlp_out_bf = _diff_attention(
            proj, diff_lambda, qw3, kw3, subln3, (w_ret_out, w_diff_out, w_out, w_mlp_in, w_mlp_out), layer, D,
            *TILES["attention"])
        merged = _merge(ret, att, w_ret_bf, w_diff_bf, proj, *TILES["merge"])
        h, hn = _out_proj_norm(merged, w_out_bf, h, norm_mlp3, layer, TILES["out_proj"])
        u = _mlp_in(hn, w_mlp_in_bf, *TILES["mlp_in"])
        h, xn = _mlp_out(u, w_mlp_out_bf, h, norm_mix3, layer + 1 if layer + 1 < depth else None, *TILES["mlp_out"])
    return h.reshape(B, L, D)
```

```python
import functools
import math

import jax
import jax.numpy as jnp
import numpy as np
from jax import lax
from jax.experimental import pallas as pl
from jax.experimental.pallas import tpu as pltpu

HEAD_DIM = 128
V_HEAD_DIM = 2 * HEAD_DIM
NORM_EPS = 1e-6
GN_EPS = 1e-5
ROPE_THETA = 10000.0
LOG2E = 1.4426950408889634
RET_CHUNK = 256
RET_UNROLL = 32
NEG_BIG = -1e30
MAX_EXP2_ARG = 50.0
SCORE_MARGIN = 1.01
VMEM_LIMIT_V7X = 56 * 1024 * 1024

TILES = {
    "rmsnorm": 512,
    "in_proj": (1024, 1024),
    "attention": (1024, 1024),
    "merge": (1024, 1024),
    "out_proj": 512,
    "mlp_in": (2048, 1024),
    "mlp_out": (1024, 1024),
}

F32 = jnp.float32
BF16 = jnp.bfloat16
_NT = (((1,), (1,)), ((), ()))
_TN = (((0,), (0,)), ((), ()))


def _params(*sem):
    return pltpu.CompilerParams(dimension_semantics=sem, vmem_limit_bytes=VMEM_LIMIT_V7X)


def _rmsnorm_kernel(x_ref, w_ref, o_ref):
    x = x_ref[...]
    ms = jnp.mean(x * x, axis=-1, keepdims=True)
    o_ref[...] = (x * lax.rsqrt(ms + NORM_EPS) * w_ref[...]).astype(o_ref.dtype)


def _rmsnorm(x, w3, layer, tm):
    L, D = x.shape
    tm = min(tm, L)
    return pl.pallas_call(
        _rmsnorm_kernel,
        grid=(L // tm,),
        in_specs=[pl.BlockSpec((tm, D), lambda i: (i, 0)),
                  pl.BlockSpec((None, 1, D), lambda i: (layer, 0, 0))],
        out_specs=pl.BlockSpec((tm, D), lambda i: (i, 0)),
        out_shape=jax.ShapeDtypeStruct((L, D), BF16),
        compiler_params=_params("arbitrary"),
        name="rmsnorm",
    )(x, w3)


def _logistic(x):
    return 0.5 * jnp.tanh(0.5 * x) + 0.5


def _proj_kernel(a_ref, w_ref, cos_ref, sin_ref, qw_ref, kw_ref, o_ref, wbf_ref, *, tn, d_model):
    j = pl.program_id(0)

    @pl.when(pl.program_id(1) == 0)
    def _cast():
        wbf_ref[...] = w_ref[...].astype(BF16)

    def matmul():
        return jnp.dot(a_ref[...], wbf_ref[...], preferred_element_type=F32)

    t = d_model // tn
    half = t // 2
    groups = tn // HEAD_DIM

    tm = a_ref.shape[0]
    rows = pl.ds(pl.multiple_of(pl.program_id(1) * tm, tm), tm)

    def rope(y):
        return y * cos_ref[rows, :] + pltpu.roll(y, HEAD_DIM // 2, 1) * sin_ref[rows, :]

    @pl.when(j < t)
    def _ret_qk():
        acc = matmul()
        s = jnp.where(j >= half, HEAD_DIM ** -0.5, 1.0).astype(F32)
        for c in range(groups):
            sl = slice(c * HEAD_DIM, (c + 1) * HEAD_DIM)
            o_ref[:, sl] = (rope(acc[:, sl]) * s).astype(o_ref.dtype)

    @pl.when(((j >= t) & (j < 2 * t)) | ((j >= 5 * t) & (j < 6 * t)))
    def _plain():
        o_ref[...] = matmul().astype(o_ref.dtype)

    @pl.when((j >= 2 * t) & (j < 3 * t))
    def _silu():
        acc = matmul()
        o_ref[...] = (acc * _logistic(acc)).astype(o_ref.dtype)

    @pl.when((j >= 3 * t) & (j < 5 * t))
    def _diff_qk():
        acc = matmul()
        is_q = (j < 4 * t).astype(F32)
        w = is_q * (qw_ref[...] * (HEAD_DIM ** -0.5 * LOG2E)) + (1.0 - is_q) * kw_ref[...]
        mean_w = jnp.full((2 * HEAD_DIM, HEAD_DIM), 1.0 / HEAD_DIM, BF16)
        for c in range(groups):
            sl = slice(c * HEAD_DIM, (c + 1) * HEAD_DIM)
            y = acc[:, sl]
            sq = y * y
            hi = sq.astype(BF16)
            lo = (sq - hi.astype(F32)).astype(BF16)
            ms = jnp.dot(jnp.concatenate([hi, lo], axis=1), mean_w, preferred_element_type=F32)
            y = y * lax.rsqrt(ms + NORM_EPS) * w
            o_ref[:, sl] = rope(y).astype(o_ref.dtype)

    @pl.when(j >= 6 * t)
    def _sigmoid():
        o_ref[...] = _logistic(matmul()).astype(o_ref.dtype)


def _in_proj(xn, w_in, cos, sin_s, qw3, kw3, layer, tm, tn):
    L, D = xn.shape
    n_cols = w_in.shape[-1]
    tm = min(tm, L)
    tn = min(tn, D // 2)
    return pl.pallas_call(
        functools.partial(_proj_kernel, tn=tn, d_model=D),
        grid=(n_cols // tn, L // tm),
        in_specs=[pl.BlockSpec((tm, D), lambda j, i: (i, 0)),
                  (pl.BlockSpec((D, tn), lambda j, i: (0, j)) if w_in.ndim == 2
                   else pl.BlockSpec((None, D, tn), lambda j, i: (layer, 0, j))),
                  pl.BlockSpec((L, HEAD_DIM), lambda j, i: (0, 0)),
                  pl.BlockSpec((L, HEAD_DIM), lambda j, i: (0, 0)),
                  pl.BlockSpec((None, 1, HEAD_DIM), lambda j, i: (layer, 0, 0)),
                  pl.BlockSpec((None, 1, HEAD_DIM), lambda j, i: (layer, 0, 0))],
        out_specs=pl.BlockSpec((tm, tn), lambda j, i: (i, j)),
        out_shape=jax.ShapeDtypeStruct((L, n_cols), BF16),
        scratch_shapes=[pltpu.VMEM((D, tn), BF16)],
        compiler_params=_params("arbitrary", "arbitrary"),
        name="in_proj",
    )(xn, w_in, cos, sin_s, qw3, kw3)


def _log_sigmoid(x):
    return jnp.minimum(x, 0.0) - jnp.log1p(jnp.exp(-jnp.abs(x)))


def _retention_kernel(lf_ref, lb_ref, q_ref, k_ref, v_ref, g_ref, gnw_ref, o_ref,
                      state_ref, dmask_ref, wq_ref, wk_ref, *, chunk, n_chunks):
    C, N = chunk, n_chunks
    lgf = _log_sigmoid(lf_ref[...])
    lgb = _log_sigmoid(lb_ref[...])
    lgf1, lgb1 = lgf[:, :1], lgb[:, :1]

    row = lax.broadcasted_iota(jnp.int32, (C, HEAD_DIM), 0).astype(F32)
    wq_ref[0] = jnp.exp((row + 1.0) * lgf)
    wq_ref[1] = jnp.exp((C - row) * lgb)
    wk_ref[0] = jnp.exp((C - 1.0 - row) * lgf)
    wk_ref[1] = jnp.exp(row * lgb)
    r = lax.broadcasted_iota(jnp.int32, (C, C), 0)
    c = lax.broadcasted_iota(jnp.int32, (C, C), 1)
    diff = (r - c).astype(F32)
    dmask_ref[...] = jnp.where(diff > 0, jnp.exp(jnp.maximum(diff, 0.0) * lgf1),
                               jnp.where(diff < 0, jnp.exp(jnp.maximum(-diff, 0.0) * lgb1), 2.0))
    dec_f = jnp.exp(C * lgf1)
    dec_b = jnp.exp(C * lgb1)

    def rows(n):
        return pl.ds(pl.multiple_of(n * C, C), C)

    def states(t, carry):
        sf, sb = carry
        nb = N - 1 - t
        kf = (k_ref[rows(t), :].astype(F32) * wk_ref[0]).astype(BF16)
        kb = (k_ref[rows(nb), :].astype(F32) * wk_ref[1]).astype(BF16)
        kvf = lax.dot_general(kf, v_ref[rows(t), :], _TN, preferred_element_type=F32)
        kvb = lax.dot_general(kb, v_ref[rows(nb), :], _TN, preferred_element_type=F32)
        state_ref[t, 0:HEAD_DIM, :] = sf.astype(BF16)
        state_ref[nb, HEAD_DIM:2 * HEAD_DIM, :] = sb.astype(BF16)
        return sf * dec_f + kvf, sb * dec_b + kvb

    zero = jnp.zeros((HEAD_DIM, V_HEAD_DIM), F32)
    lax.fori_loop(0, N, states, (zero, zero), unroll=min(N, RET_UNROLL))

    def chunk_out(n, carry):
        q = q_ref[rows(n), :]
        s = lax.dot_general(q, k_ref[rows(n), :], _NT, preferred_element_type=F32)
        sd = (s * dmask_ref[...]).astype(BF16)
        qf = q.astype(F32)
        qcat = jnp.concatenate([qf * wq_ref[0], qf * wq_ref[1]], axis=1).astype(BF16)
        o = (jnp.dot(sd, v_ref[rows(n), :], preferred_element_type=F32)
             + jnp.dot(qcat, state_ref[n], preferred_element_type=F32))
        mu = jnp.mean(o, axis=-1, keepdims=True)
        xc = o - mu
        var = jnp.mean(xc * xc, axis=-1, keepdims=True)
        y = xc * lax.rsqrt(var + GN_EPS) * gnw_ref[...]
        o_ref[rows(n), :] = (g_ref[rows(n), :].astype(F32) * y).astype(o_ref.dtype)
        return carry

    lax.fori_loop(0, N, chunk_out, 0, unroll=min(N, RET_UNROLL))


def _retention(proj, lf4, lb4, gnw4, layer, d_model):
    L = proj.shape[0]
    H = d_model // V_HEAD_DIM
    C = min(RET_CHUNK, L)
    N = L // C
    qk_blocks = d_model // 2 // HEAD_DIM
    v_blocks = d_model // V_HEAD_DIM
    return pl.pallas_call(
        functools.partial(_retention_kernel, chunk=C, n_chunks=N),
        grid=(H,),
        in_specs=[pl.BlockSpec((None, None, 1, HEAD_DIM), lambda h: (layer, h, 0, 0)),
                  pl.BlockSpec((None, None, 1, HEAD_DIM), lambda h: (layer, h, 0, 0)),
                  pl.BlockSpec((L, HEAD_DIM), lambda h: (0, h)),
                  pl.BlockSpec((L, HEAD_DIM), lambda h: (0, qk_blocks + h)),
                  pl.BlockSpec((L, V_HEAD_DIM), lambda h: (0, v_blocks + h)),
                  pl.BlockSpec((L, V_HEAD_DIM), lambda h: (0, 2 * v_blocks + h)),
                  pl.BlockSpec((None, None, 1, V_HEAD_DIM), lambda h: (layer, h, 0, 0))],
        out_specs=pl.BlockSpec((L, V_HEAD_DIM), lambda h: (0, h)),
        out_shape=jax.ShapeDtypeStruct((L, d_model), BF16),
        scratch_shapes=[pltpu.VMEM((N, 2 * HEAD_DIM, V_HEAD_DIM), BF16),
                        pltpu.VMEM((C, C), F32),
                        pltpu.VMEM((2, C, HEAD_DIM), F32),
                        pltpu.VMEM((2, C, HEAD_DIM), F32)],
        compiler_params=_params("arbitrary"),
        name="retention",
    )(lf4, lb4, proj, proj, proj, proj, gnw4)


def _attn_kernel(lam_ref, qw_ref, kw_ref, q_ref, k_ref, v_ref, w_ref, *rest, tk, n_kv, lambda_init, n_cast):
    cast_in, o_ref, cast_out = rest[:n_cast], rest[n_cast], rest[n_cast + 1:2 * n_cast + 1]
    acc_ref, l_ref = rest[2 * n_cast + 1:]
    tq = q_ref.shape[0]
    q = q_ref[...]
    qs = (q[:, :HEAD_DIM], q[:, HEAD_DIM:])

    def cast_weight_slabs():
        for src, dst in zip(cast_in, cast_out):
            dst[...] = src[...].astype(BF16)

    score_bound = (HEAD_DIM * (HEAD_DIM ** -0.5 * LOG2E) * SCORE_MARGIN
                   * jnp.max(jnp.abs(qw_ref[...])) * jnp.max(jnp.abs(kw_ref[...])))

    def scores(t, k):
        return lax.dot_general(qs[t], k[:, t * HEAD_DIM:(t + 1) * HEAD_DIM], _NT, preferred_element_type=F32)

    def finish(l1, l2):
        lp = lam_ref[...]
        lam = (jnp.exp(jnp.sum(lp[0:1] * lp[1:2], axis=-1, keepdims=True))
               - jnp.exp(jnp.sum(lp[2:3] * lp[3:4], axis=-1, keepdims=True)) + lambda_init)
        o = acc_ref[0] * (1.0 / l1) - lam * (acc_ref[1] * (1.0 / l2))
        ms = jnp.mean(o * o, axis=-1, keepdims=True)
        y = o * lax.rsqrt(ms + NORM_EPS) * w_ref[...] * (1.0 - lambda_init)
        o_ref[...] = y.astype(o_ref.dtype)

    @pl.when(score_bound <= MAX_EXP2_ARG)
    def _bounded():
        cast_weight_slabs()
        for j in range(n_kv):
            k = k_ref[j * tk:(j + 1) * tk, :]
            v = v_ref[j * tk:(j + 1) * tk, :]
            for t in range(2):
                p = jnp.exp2(scores(t, k))
                lsum = p[:, :HEAD_DIM]
                for c in range(1, tk // HEAD_DIM):
                    lsum = lsum + p[:, c * HEAD_DIM:(c + 1) * HEAD_DIM]
                pv = jnp.dot(p.astype(BF16), v, preferred_element_type=F32)
                if j == 0:
                    l_ref[t] = lsum
                    acc_ref[t] = pv
                else:
                    l_ref[t] += lsum
                    acc_ref[t] += pv
        finish(jnp.sum(l_ref[0], axis=-1, keepdims=True), jnp.sum(l_ref[1], axis=-1, keepdims=True))

    @pl.when(score_bound > MAX_EXP2_ARG)
    def _online():
        cast_weight_slabs()
        acc_ref[...] = jnp.zeros_like(acc_ref)

        def body(j, carry):
            rows = pl.ds(pl.multiple_of(j * tk, tk), tk)
            k = k_ref[rows, :]
            v = v_ref[rows, :]
            out = []
            for t in range(2):
                m, l = carry[2 * t], carry[2 * t + 1]
                s = scores(t, k)
                m_new = jnp.maximum(m, jnp.max(s, axis=-1, keepdims=True))
                alpha = jnp.exp2(m - m_new)
                p = jnp.exp2(s - m_new)
                l_new = alpha * l + jnp.sum(p, axis=-1, keepdims=True)
                acc_ref[t] = alpha * acc_ref[t] + jnp.dot(p.astype(BF16), v, preferred_element_type=F32)
                out += [m_new, l_new]
            return tuple(out)

        m0 = jnp.full((tq, 1), NEG_BIG, F32)
        l0 = jnp.zeros((tq, 1), F32)
        _, l1, _, l2 = lax.fori_loop(0, n_kv, body, (m0, l0, m0, l0))
        finish(l1, l2)


def _diff_attention(proj, diff_lambda, qw3, kw3, subln3, weights, layer, d_model, tq, tk):
    L = proj.shape[0]
    H = d_model // V_HEAD_DIM
    tq = min(tq, L)
    tk = min(tk, L)
    vb = d_model // V_HEAD_DIM
    lambda_init = 0.8 - 0.6 * math.exp(-0.3 * layer)
    nq = L // tq
    steps = H * nq
    cast_in, cast_out, cast_shapes = [], [], []
    for w, w_layer in weights:
        rows, cols = w.shape[1:]
        slab = rows // steps
        cast_in.append(pl.BlockSpec((None, slab, cols), lambda h, i, w_layer=w_layer: (w_layer, h * nq + i, 0)))
        cast_out.append(pl.BlockSpec((slab, cols), lambda h, i: (h * nq + i, 0)))
        cast_shapes.append(jax.ShapeDtypeStruct((rows, cols), BF16))
    return pl.pallas_call(
        functools.partial(_attn_kernel, tk=tk, n_kv=L // tk, lambda_init=lambda_init, n_cast=len(weights)),
        grid=(H, L // tq),
        in_specs=[pl.BlockSpec((None, 4, HEAD_DIM), lambda h, i: (layer, 0, 0)),
                  pl.BlockSpec((None, 1, HEAD_DIM), lambda h, i: (layer, 0, 0)),
                  pl.BlockSpec((None, 1, HEAD_DIM), lambda h, i: (layer, 0, 0)),
                  pl.BlockSpec((tq, V_HEAD_DIM), lambda h, i: (i, 3 * vb + h)),
                  pl.BlockSpec((L, V_HEAD_DIM), lambda h, i: (0, 4 * vb + h)),
                  pl.BlockSpec((L, V_HEAD_DIM), lambda h, i: (0, 5 * vb + h)),
                  pl.BlockSpec((None, 1, V_HEAD_DIM), lambda h, i: (layer, 0, 0))] + cast_in,
        out_specs=[pl.BlockSpec((tq, V_HEAD_DIM), lambda h, i: (i, h))] + cast_out,
        out_shape=[jax.ShapeDtypeStruct((L, d_model), BF16)] + cast_shapes,
        scratch_shapes=[pltpu.VMEM((2, tq, V_HEAD_DIM), F32),
                        pltpu.VMEM((2, tq, HEAD_DIM), F32)],
        compiler_params=_params("arbitrary", "arbitrary"),
        name="diff_attention",
    )(diff_lambda, qw3, kw3, proj, proj, proj, subln3, *[w for w, _ in weights])


def _merge_kernel(a1_ref, a2_ref, w1_ref, w2_ref, g1_ref, g2_ref, o_ref):
    y1 = jnp.dot(a1_ref[...], w1_ref[...], preferred_element_type=F32)
    y2 = jnp.dot(a2_ref[...], w2_ref[...], preferred_element_type=F32)
    o_ref[...] = (g1_ref[...].astype(F32) * y1 + g2_ref[...].astype(F32) * y2).astype(o_ref.dtype)


def _merge(ret, att, w_ret_out, w_diff_out, proj, tm, tn):
    L, D = ret.shape
    tm = min(tm, L)
    tn = min(tn, D)
    gb = D // tn
    return pl.pallas_call(
        _merge_kernel,
        grid=(D // tn, L // tm),
        in_specs=[pl.BlockSpec((tm, D), lambda j, i: (i, 0)),
                  pl.BlockSpec((tm, D), lambda j, i: (i, 0)),
                  pl.BlockSpec((D, tn), lambda j, i: (0, j)),
                  pl.BlockSpec((D, tn), lambda j, i: (0, j)),
                  pl.BlockSpec((tm, tn), lambda j, i: (i, 6 * gb + j)),
                  pl.BlockSpec((tm, tn), lambda j, i: (i, 7 * gb + j))],
        out_specs=pl.BlockSpec((tm, tn), lambda j, i: (i, j)),
        out_shape=jax.ShapeDtypeStruct((L, D), BF16),
        compiler_params=_params("arbitrary", "arbitrary"),
        name="merge",
    )(ret, att, w_ret_out, w_diff_out, proj, proj)


def _out_proj_norm_kernel(a_ref, w_ref, x_ref, nw_ref, o_ref, hn_ref):
    y = x_ref[...] + jnp.dot(a_ref[...], w_ref[...], preferred_element_type=F32)
    o_ref[...] = y
    ms = jnp.mean(y * y, axis=-1, keepdims=True)
    hn_ref[...] = (y * lax.rsqrt(ms + NORM_EPS) * nw_ref[...]).astype(hn_ref.dtype)


def _out_proj_norm(a, w, x, nw3, layer, tm):
    L, D = x.shape
    tm = min(tm, L)
    return pl.pallas_call(
        _out_proj_norm_kernel,
        grid=(L // tm,),
        in_specs=[pl.BlockSpec((tm, D), lambda i: (i, 0)),
                  pl.BlockSpec((D, D), lambda i: (0, 0)),
                  pl.BlockSpec((tm, D), lambda i: (i, 0)),
                  pl.BlockSpec((None, 1, D), lambda i: (layer, 0, 0))],
        out_specs=[pl.BlockSpec((tm, D), lambda i: (i, 0)),
                   pl.BlockSpec((tm, D), lambda i: (i, 0))],
        out_shape=[jax.ShapeDtypeStruct((L, D), F32), jax.ShapeDtypeStruct((L, D), BF16)],
        compiler_params=_params("arbitrary"),
        name="out_proj",
    )(a, w, x, nw3)


def _mlp_in_kernel(a_ref, w_ref, o_ref):
    y = jnp.dot(a_ref[...], w_ref[...], preferred_element_type=F32)
    o_ref[...] = jnp.square(jnp.maximum(y, 0.0)).astype(o_ref.dtype)


def _mlp_in(a, w, tm, tn):
    L, K = a.shape
    n_cols = w.shape[-1]
    tm = min(tm, L)
    tn = min(tn, n_cols)
    return pl.pallas_call(
        _mlp_in_kernel,
        grid=(n_cols // tn, L // tm),
        in_specs=[pl.BlockSpec((tm, K), lambda j, i: (i, 0)),
                  pl.BlockSpec((K, tn), lambda j, i: (0, j))],
        out_specs=pl.BlockSpec((tm, tn), lambda j, i: (i, j)),
        out_shape=jax.ShapeDtypeStruct((L, n_cols), BF16),
        compiler_params=_params("arbitrary", "arbitrary"),
        name="mlp_in",
    )(a, w)


def _mlp_out_kernel(a_ref, w_ref, x_ref, *rest, emit_norm):
    if emit_norm:
        nw_ref, o_ref, xn_ref = rest
    else:
        (o_ref,) = rest
    k = pl.program_id(1)

    @pl.when(k == 0)
    def _init():
        o_ref[...] = x_ref[...]

    o_ref[...] += jnp.dot(a_ref[...], w_ref[...], preferred_element_type=F32)

    if emit_norm:
        @pl.when(k == pl.num_programs(1) - 1)
        def _norm():
            y = o_ref[...]
            ms = jnp.mean(y * y, axis=-1, keepdims=True)
            xn_ref[...] = (y * lax.rsqrt(ms + NORM_EPS) * nw_ref[...]).astype(xn_ref.dtype)


def _mlp_out(a, w, x, nw3, next_layer, tm, tk):
    L, K = a.shape
    D = w.shape[-1]
    tm, tk = min(tm, L), min(tk, K)
    emit_norm = next_layer is not None
    in_specs = [pl.BlockSpec((tm, tk), lambda i, k: (i, k)),
                pl.BlockSpec((tk, D), lambda i, k: (k, 0)),
                pl.BlockSpec((tm, D), lambda i, k: (i, 0))]
    out_specs = [pl.BlockSpec((tm, D), lambda i, k: (i, 0))]
    out_shape = [jax.ShapeDtypeStruct((L, D), F32)]
    args = [a, w, x]
    if emit_norm:
        in_specs.append(pl.BlockSpec((None, 1, D), lambda i, k: (next_layer, 0, 0)))
        out_specs.append(pl.BlockSpec((tm, D), lambda i, k: (i, 0)))
        out_shape.append(jax.ShapeDtypeStruct((L, D), BF16))
        args.append(nw3)
    out = pl.pallas_call(
        functools.partial(_mlp_out_kernel, emit_norm=emit_norm),
        grid=(L // tm, K // tk),
        in_specs=in_specs,
        out_specs=out_specs,
        out_shape=out_shape,
        compiler_params=_params("arbitrary", "arbitrary"),
        name="mlp_out",
    )(*args)
    return (out[0], out[1]) if emit_norm else (out[0], None)


def _rope_tables(L):
    inv = 1.0 / (np.float32(ROPE_THETA) ** (np.arange(0, HEAD_DIM, 2, dtype=np.float32) / np.float32(HEAD_DIM)))
    ang = np.arange(L, dtype=np.float32)[:, None] * inv.astype(np.float32)[None, :]
    ang = np.concatenate([ang, ang], axis=-1)
    sign = np.where(np.arange(HEAD_DIM) < HEAD_DIM // 2, -1.0, 1.0).astype(np.float32)
    return jnp.asarray(np.cos(ang), F32), jnp.asarray(np.sin(ang) * sign[None, :], F32)


def kernel(x, norm_mix_w, w_in, ret_decay_fwd, ret_decay_bwd, ret_gn_w, w_ret_out, q_norm_w, k_norm_w,
           diff_lambda, diff_subln_w, w_diff_out, w_out, norm_mlp_w, w_mlp_in, w_mlp_out):
    B, L, D = x.shape
    assert B == 1 and D % V_HEAD_DIM == 0
    depth = w_in.shape[0]
    H = D // V_HEAD_DIM
    cos, sin_s = _rope_tables(L)

    norm_mix3 = norm_mix_w.reshape(depth, 1, D)
    norm_mlp3 = norm_mlp_w.reshape(depth, 1, D)
    qw3 = q_norm_w.reshape(depth, 1, HEAD_DIM)
    kw3 = k_norm_w.reshape(depth, 1, HEAD_DIM)
    subln3 = diff_subln_w.reshape(depth, 1, V_HEAD_DIM)
    gnw4 = ret_gn_w.reshape(depth, H, 1, V_HEAD_DIM)
    lf4 = jnp.broadcast_to(ret_decay_fwd[:, :, None, None], (depth, H, 1, HEAD_DIM))
    lb4 = jnp.broadcast_to(ret_decay_bwd[:, :, None, None], (depth, H, 1, HEAD_DIM))

    h = x.reshape(L, D)
    xn = _rmsnorm(h, norm_mix3, 0, TILES["rmsnorm"])
    w_in_next = []
    for layer in range(depth):
        w_proj = w_in_next[0] if w_in_next else w_in
        proj = _in_proj(xn, w_proj, cos, sin_s, qw3, kw3, layer, *TILES["in_proj"])
        ret = _retention(proj, lf4, lb4, gnw4, layer, D)
        to_cast = [(w, layer) for w in (w_ret_out, w_diff_out, w_out, w_mlp_in, w_mlp_out)]
        if layer + 1 < depth:
            to_cast.append((w_in, layer + 1))
        att, w_ret_bf, w_diff_bf, w_out_bf, w_mlp_in_bf, w_mlp_out_bf, *w_in_next = _diff_attention(
            proj, diff_lambda, qw3, kw3, subln3, to_cast, layer, D, *TILES["attention"])
        merged = _merge(ret, att, w_ret_bf, w_diff_bf, proj, *TILES["merge"])
        h, hn = _out_proj_norm(merged, w_out_bf, h, norm_mlp3, layer, TILES["out_proj"])
        u = _mlp_in(hn, w_mlp_in_bf, *TILES["mlp_in"])
        h, xn = _mlp_out(u, w_mlp_out_bf, h, norm_mix3, layer + 1 if layer + 1 < depth else None, *TILES["mlp_out"])
    return h.reshape(B, L, D)
```
